```python
import math
import jax, jax.numpy as jnp
from jax import lax
import numpy as np

D_MODEL = 1024
BATCH = 8
SEQ = 8192
DEPTH = 2
DEC_BATCH = 16
DEC_SEQ = 32
PAST_LEN = 1024

CHUNK = 64
N_MIXERS = 2
N_SSD_LAYERS = (DEPTH + 1) // 2
N_MLA_LAYERS = DEPTH // 2
NORM_EPS = 1e-6

SSD_EXPAND = 2
SSD_D_INNER = SSD_EXPAND * D_MODEL
SSD_HEAD_DIM = 64
SSD_HEADS = SSD_D_INNER // SSD_HEAD_DIM
SSD_GROUPS = 8
SSD_HEADS_PER_GROUP = SSD_HEADS // SSD_GROUPS
SSD_STATE = 128
SSD_CONV = 4
SSD_GN = SSD_GROUPS * SSD_STATE
SSD_CONV_DIM = SSD_D_INNER + 2 * SSD_GN
SSD_IN_DIM = SSD_D_INNER + SSD_CONV_DIM + SSD_HEADS

MLA_HEADS = 16
MLA_Q_LORA = 512
MLA_KV_LORA = 256
MLA_NOPE = 64
MLA_ROPE = 32
MLA_V = 64
MLA_SCALE = 1.0 / math.sqrt(MLA_NOPE + MLA_ROPE)
ROPE_THETA = 10000.0
Q_BLOCK = 128

FFN_HIDDEN = ((8 * D_MODEL + 3 * 256 - 1) // (3 * 256)) * 256

kernel_name = 'ssd_mla_sandwich_stream_step'


def _rmsnorm(x, g):
    xf = x.astype(jnp.float32)
    r = lax.rsqrt(jnp.mean(xf * xf, axis=-1, keepdims=True) + NORM_EPS)
    return (xf * r).astype(x.dtype) * g


def _rope(t, pos):
    half = MLA_ROPE // 2
    inv = ROPE_THETA ** (-jnp.arange(half, dtype=jnp.float32) / half)
    ang = pos.astype(jnp.float32)[:, None] * inv[None, :]
    cos = jnp.cos(ang)[None, :, None, :]
    sin = jnp.sin(ang)[None, :, None, :]
    tf = t.astype(jnp.float32)
    t1, t2 = tf[..., :half], tf[..., half:]
    return jnp.concatenate([t1 * cos - t2 * sin, t1 * sin + t2 * cos], axis=-1).astype(t.dtype)


def _ssd_scan(x, dt, a, bm, cm, s0):
    b, l = x.shape[:2]
    pad = (-l) % CHUNK
    f32 = jnp.float32

    def prep(t):
        t = t.astype(f32)
        return jnp.pad(t, [(0, 0), (0, pad)] + [(0, 0)] * (t.ndim - 2))

    lp = l + pad
    nc = lp // CHUNK
    G, R, P, N = SSD_GROUPS, SSD_HEADS_PER_GROUP, SSD_HEAD_DIM, SSD_STATE

    def to_chunks(t):
        return jnp.moveaxis(t.reshape((b, nc, CHUNK) + t.shape[2:]), 1, 0)

    xs = (to_chunks(prep(x).reshape(b, lp, G, R, P)),
          to_chunks(prep(dt).reshape(b, lp, G, R)),
          to_chunks(prep(bm)),
          to_chunks(prep(cm)))
    ag = a.astype(f32).reshape(G, R)
    causal = jnp.tril(jnp.ones((CHUNK, CHUNK), dtype=bool))[None, :, :, None, None]

    def step(state, inp):
        xc, dtc, bc, cc = inp
        acum = jnp.cumsum(dtc * ag, axis=1)
        seg = jnp.where(causal, acum[:, :, None] - acum[:, None, :], -jnp.inf)
        decay = jnp.exp(seg)
        cb = jnp.einsum('bqgn,bsgn->bqsg', cc, bc)
        y_in = jnp.einsum('bqsg,bqsgr,bsgr,bsgrp->bqgrp', cb, decay, dtc, xc)
        y_state = jnp.einsum('bqgn,bgrpn,bqgr->bqgrp', cc, state, jnp.exp(acum))
        w_end = jnp.exp(acum[:, -1:] - acum) * dtc
        new_state = (jnp.exp(acum[:, -1])[..., None, None] * state
                     + jnp.einsum('bsgn,bsgr,bsgrp->bgrpn', bc, w_end, xc))
        return new_state, y_in + y_state

    s_init = s0.astype(f32).reshape(b, G, R, P, N)
    s_fin, ys = lax.scan(step, s_init, xs)
    y = jnp.moveaxis(ys, 0, 1).reshape(b, lp, SSD_HEADS, P)[:, :l]
    return y.astype(x.dtype), s_fin.reshape(b, SSD_HEADS, P, N).astype(s0.dtype)


def _ssd_mixer(a, conv_buf, ssm_state, w_in, conv_w, conv_b, dt_bias, a_log, d_skip, gate_norm, w_out):
    b, l, _ = a.shape
    proj = a @ w_in
    z = proj[..., :SSD_D_INNER]
    xbc = proj[..., SSD_D_INNER:SSD_D_INNER + SSD_CONV_DIM]
    dt_raw = proj[..., SSD_D_INNER + SSD_CONV_DIM:]
    xp = jnp.concatenate([conv_buf.astype(xbc.dtype), xbc], axis=1)
    new_buf = xp[:, -(SSD_CONV - 1):]
    conv = lax.conv_general_dilated(xp, conv_w[:, None, :].astype(xp.dtype), (1,), 'VALID',
                                    dimension_numbers=('NWC', 'WIO', 'NWC'),
                                    feature_group_count=SSD_CONV_DIM)
    xbc = jax.nn.silu(conv + conv_b)
    xs = xbc[..., :SSD_D_INNER].reshape(b, l, SSD_HEADS, SSD_HEAD_DIM)
    bm = xbc[..., SSD_D_INNER:SSD_D_INNER + SSD_GN].reshape(b, l, SSD_GROUPS, SSD_STATE)
    cm = xbc[..., SSD_D_INNER + SSD_GN:].reshape(b, l, SSD_GROUPS, SSD_STATE)
    dt = jax.nn.softplus((dt_raw + dt_bias).astype(jnp.float32))
    y, new_state = _ssd_scan(xs, dt, -jnp.exp(a_log.astype(jnp.float32)), bm, cm, ssm_state)
    y = y + d_skip[:, None] * xs
    y = (y.reshape(b, l, SSD_D_INNER) * jax.nn.silu(z)).reshape(b, l, SSD_GROUPS, SSD_D_INNER // SSD_GROUPS)
    y = _rmsnorm(y, gate_norm.reshape(SSD_GROUPS, -1)).reshape(b, l, SSD_D_INNER)
    return y @ w_out, new_buf, new_state


def _mla_attend(qn, qr, qpos, k_lat, k_rope, kpos, w_uk, w_uv):
    q_lat = jnp.einsum('bqhd,chd->bqhc', qn, w_uk)
    s = jnp.einsum('bqhc,bkc->bhqk', q_lat, k_lat) + jnp.einsum('bqhr,bkr->bhqk', qr, k_rope)
    s = s.astype(jnp.float32) * MLA_SCALE
    visible = (kpos[None, :] // CHUNK) <= (qpos[:, None] // CHUNK)
    p = jax.nn.softmax(jnp.where(visible, s, -jnp.inf), axis=-1).astype(k_lat.dtype)
    o_lat = jnp.einsum('bhqk,bkc->bqhc', p, k_lat)
    return jnp.einsum('bqhc,chv->bqhv', o_lat, w_uv)


def _mla_mixer(a, pos, past_lat, past_rope, past_pos, wq_a, q_norm, wq_b, wkv_a, kv_norm, w_uk, w_uv, w_o):
    b, l, _ = a.shape
    q = (_rmsnorm(a @ wq_a, q_norm) @ wq_b).reshape(b, l, MLA_HEADS, MLA_NOPE + MLA_ROPE)
    qn = q[..., :MLA_NOPE]
    qr = _rope(q[..., MLA_NOPE:], pos)
    kv = a @ wkv_a
    lat = _rmsnorm(kv[..., :MLA_KV_LORA], kv_norm)
    kr = _rope(kv[..., None, MLA_KV_LORA:], pos)[:, :, 0]
    k_lat = jnp.concatenate([past_lat.astype(lat.dtype), lat], axis=1)
    k_rope = jnp.concatenate([past_rope.astype(kr.dtype), kr], axis=1)
    kpos = jnp.concatenate([past_pos, pos])
    nblk = -(-l // Q_BLOCK)
    if nblk == 1:
        o = _mla_attend(qn, qr, pos, k_lat, k_rope, kpos, w_uk, w_uv)
    else:
        def blocks(t):
            return jnp.moveaxis(t.reshape((b, nblk, Q_BLOCK) + t.shape[2:]), 1, 0)
        ob = lax.map(lambda blk: _mla_attend(blk[0], blk[1], blk[2], k_lat, k_rope, kpos, w_uk, w_uv),
                     (blocks(qn), blocks(qr), pos.reshape(nblk, Q_BLOCK)))
        o = jnp.moveaxis(ob, 0, 1).reshape(b, l, MLA_HEADS, MLA_V)
    return o.reshape(b, l, MLA_HEADS * MLA_V) @ w_o, lat, kr


def _swiglu(a, w_gate, w_up, w_down):
    return (jax.nn.silu(a @ w_gate) * (a @ w_up)) @ w_down


def _trunk(x, pos, conv_bufs, ssm_states, past_lat, past_rope, past_pos, p):
    h = x
    new_conv, new_ssm, new_lat, new_rope = [], [], [], []
    for i in range(DEPTH):
        j = i // N_MIXERS
        a = _rmsnorm(h, p['ln_mix_pre'][i])
        if i % N_MIXERS == 0:
            m, cb, st = _ssd_mixer(a, conv_bufs[j], ssm_states[j], p['ssd_w_in'][j], p['ssd_conv_w'][j],
                                   p['ssd_conv_b'][j], p['ssd_dt_bias'][j], p['ssd_a_log'][j], p['ssd_d'][j],
                                   p['ssd_gate_norm'][j], p['ssd_w_out'][j])
            new_conv.append(cb)
            new_ssm.append(st)
        else:
            m, lat, kr = _mla_mixer(a, pos, past_lat[j], past_rope[j], past_pos, p['mla_wq_a'][j],
                                    p['mla_q_norm'][j], p['mla_wq_b'][j], p['mla_wkv_a'][j], p['mla_kv_norm'][j],
                                    p['mla_w_uk'][j], p['mla_w_uv'][j], p['mla_w_o'][j])
            new_lat.append(lat)
            new_rope.append(kr)
        h = h + _rmsnorm(m, p['ln_mix_post'][i])
        f = _swiglu(_rmsnorm(h, p['ln_ffn_pre'][i]), p['ffn_w_gate'][i], p['ffn_w_up'][i], p['ffn_w_down'][i])
        h = h + _rmsnorm(f, p['ln_ffn_post'][i])
    return h, jnp.stack(new_conv), jnp.stack(new_ssm), jnp.stack(new_lat), jnp.stack(new_rope)


def setup_inputs(seed: int = 0) -> dict:
    key = jax.random.key(seed)
    ks = iter(jax.random.split(key, 40))
    f32 = jnp.float32

    def nrm(shape, scale):
        return jax.random.normal(next(ks), shape, f32) * scale

    def gain(shape):
        return 1.0 + 0.05 * jax.random.normal(next(ks), shape, f32)

    u = jax.random.uniform(next(ks), (N_SSD_LAYERS, SSD_HEADS), f32)
    dt0 = jnp.exp(u * (math.log(0.1) - math.log(0.001)) + math.log(0.001))
    dt_bias = dt0 + jnp.log(-jnp.expm1(-dt0))
    a_log = jnp.log(jax.random.uniform(next(ks), (N_SSD_LAYERS, SSD_HEADS), f32, 1.0, 16.0))
    return {
        'x_prompt': nrm((BATCH, SEQ, D_MODEL), 1.0),
        'x_sample': nrm((DEC_BATCH, DEC_SEQ, D_MODEL), 1.0),
        'state_ssd_conv': nrm((N_SSD_LAYERS, DEC_BATCH, SSD_CONV - 1, SSD_CONV_DIM), 1.0),
        'state_ssd_ssm': nrm((N_SSD_LAYERS, DEC_BATCH, SSD_HEADS, SSD_HEAD_DIM, SSD_STATE), 0.1),
        'cache_mla_latent': nrm((N_MLA_LAYERS, DEC_BATCH, PAST_LEN, MLA_KV_LORA), 1.0),
        'cache_mla_krope': nrm((N_MLA_LAYERS, DEC_BATCH, PAST_LEN, MLA_ROPE), 1.0),
        'ln_mix_pre': gain((DEPTH, D_MODEL)),
        'ln_mix_post': gain((DEPTH, D_MODEL)),
        'ln_ffn_pre': gain((DEPTH, D_MODEL)),
        'ln_ffn_post': gain((DEPTH, D_MODEL)),
        'ssd_w_in': nrm((N_SSD_LAYERS, D_MODEL, SSD_IN_DIM), D_MODEL ** -0.5),
        'ssd_conv_w': nrm((N_SSD_LAYERS, SSD_CONV, SSD_CONV_DIM), SSD_CONV ** -0.5),
        'ssd_conv_b': nrm((N_SSD_LAYERS, SSD_CONV_DIM), 0.01),
        'ssd_dt_bias': dt_bias,
        'ssd_a_log': a_log,
        'ssd_d': gain((N_SSD_LAYERS, SSD_HEADS)),
        'ssd_gate_norm': gain((N_SSD_LAYERS, SSD_D_INNER)),
        'ssd_w_out': nrm((N_SSD_LAYERS, SSD_D_INNER, D_MODEL), SSD_D_INNER ** -0.5),
        'mla_wq_a': nrm((N_MLA_LAYERS, D_MODEL, MLA_Q_LORA), D_MODEL ** -0.5),
        'mla_q_norm': gain((N_MLA_LAYERS, MLA_Q_LORA)),
        'mla_wq_b': nrm((N_MLA_LAYERS, MLA_Q_LORA, MLA_HEADS * (MLA_NOPE + MLA_ROPE)), MLA_Q_LORA ** -0.5),
        'mla_wkv_a': nrm((N_MLA_LAYERS, D_MODEL, MLA_KV_LORA + MLA_ROPE), D_MODEL ** -0.5),
        'mla_kv_norm': gain((N_MLA_LAYERS, MLA_KV_LORA)),
        'mla_w_uk': nrm((N_MLA_LAYERS, MLA_KV_LORA, MLA_HEADS, MLA_NOPE), MLA_KV_LORA ** -0.5),
        'mla_w_uv': nrm((N_MLA_LAYERS, MLA_KV_LORA, MLA_HEADS, MLA_V), MLA_KV_LORA ** -0.5),
        'mla_w_o': nrm((N_MLA_LAYERS, MLA_HEADS * MLA_V, D_MODEL), (MLA_HEADS * MLA_V) ** -0.5),
        'ffn_w_gate': nrm((DEPTH, D_MODEL, FFN_HIDDEN), D_MODEL ** -0.5),
        'ffn_w_up': nrm((DEPTH, D_MODEL, FFN_HIDDEN), D_MODEL ** -0.5),
        'ffn_w_down': nrm((DEPTH, FFN_HIDDEN, D_MODEL), FFN_HIDDEN ** -0.5),
    }


def reference(x_prompt, x_sample, state_ssd_conv, state_ssd_ssm, cache_mla_latent, cache_mla_krope,
              ln_mix_pre, ln_mix_post, ln_ffn_pre, ln_ffn_post,
              ssd_w_in, ssd_conv_w, ssd_conv_b, ssd_dt_bias, ssd_a_log, ssd_d, ssd_gate_norm, ssd_w_out,
              mla_wq_a, mla_q_norm, mla_wq_b, mla_wkv_a, mla_kv_norm, mla_w_uk, mla_w_uv, mla_w_o,
              ffn_w_gate, ffn_w_up, ffn_w_down):
    p = dict(ln_mix_pre=ln_mix_pre, ln_mix_post=ln_mix_post, ln_ffn_pre=ln_ffn_pre, ln_ffn_post=ln_ffn_post,
             ssd_w_in=ssd_w_in, ssd_conv_w=ssd_conv_w, ssd_conv_b=ssd_conv_b, ssd_dt_bias=ssd_dt_bias,
             ssd_a_log=ssd_a_log, ssd_d=ssd_d, ssd_gate_norm=ssd_gate_norm, ssd_w_out=ssd_w_out,
             mla_wq_a=mla_wq_a, mla_q_norm=mla_q_norm, mla_wq_b=mla_wq_b, mla_wkv_a=mla_wkv_a,
             mla_kv_norm=mla_kv_norm, mla_w_uk=mla_w_uk, mla_w_uv=mla_w_uv, mla_w_o=mla_w_o,
             ffn_w_gate=ffn_w_gate, ffn_w_up=ffn_w_up, ffn_w_down=ffn_w_down)
    b, l = x_prompt.shape[:2]
    dt_ = x_prompt.dtype
    y_prompt, p_conv, p_ssm, p_latent, p_krope = _trunk(
        x_prompt, jnp.arange(l, dtype=jnp.int32),
        jnp.zeros((N_SSD_LAYERS, b, SSD_CONV - 1, SSD_CONV_DIM), dt_),
        jnp.zeros((N_SSD_LAYERS, b, SSD_HEADS, SSD_HEAD_DIM, SSD_STATE), dt_),
        jnp.zeros((N_MLA_LAYERS, b, 0, MLA_KV_LORA), dt_),
        jnp.zeros((N_MLA_LAYERS, b, 0, MLA_ROPE), dt_),
        jnp.zeros((0,), jnp.int32), p)
    ls = x_sample.shape[1]
    past_len = cache_mla_latent.shape[2]
    y_sample, s_conv, s_ssm, s_latent, s_krope = _trunk(
        x_sample, past_len + jnp.arange(ls, dtype=jnp.int32),
        state_ssd_conv, state_ssd_ssm, cache_mla_latent, cache_mla_krope,
        jnp.arange(past_len, dtype=jnp.int32), p)
    return (y_prompt, y_sample, p_conv, p_ssm, p_latent, p_krope, s_conv, s_ssm, s_latent, s_krope)
```

```python
import functools
import math

import jax
import jax.numpy as jnp
from jax import lax
from jax.experimental import pallas as pl
from jax.experimental.pallas import tpu as pltpu

F32 = jnp.float32
BF16 = jnp.bfloat16

D_MODEL = 1024
NORM_EPS = 1e-6
CHUNK = 64

SSD_D_INNER = 2048
SSD_HEAD_DIM = 64
SSD_HEADS = 32
SSD_GROUPS = 8
SSD_HEADS_PER_GROUP = 4
SSD_STATE = 128
SSD_CONV = 4
SSD_GN = SSD_GROUPS * SSD_STATE
SSD_CONV_DIM = SSD_D_INNER + 2 * SSD_GN
SSD_ZX_DIM = SSD_D_INNER + SSD_CONV_DIM
SSD_GROUP_WIDTH = SSD_HEADS_PER_GROUP * SSD_HEAD_DIM

MLA_HEADS = 16
MLA_Q_LORA = 512
MLA_KV_LORA = 256
MLA_NOPE = 64
MLA_ROPE = 32
MLA_ROPE_HALF = MLA_ROPE // 2
MLA_QK = MLA_NOPE + MLA_ROPE
MLA_V = 64
MLA_SCALE = 1.0 / math.sqrt(MLA_QK)
ROPE_THETA = 10000.0

FFN_HIDDEN = 2816

LANES = 128
MXU_WIDTH = 256
VMEM_LIMIT_BYTES = 56 * 1024 * 1024

HEAD_SLOT = LANES
ROPE_LO = MLA_NOPE
ROPE_HI = MLA_NOPE + MLA_ROPE_HALF
MLA_Q_WIDTH = MLA_HEADS * HEAD_SLOT
MLA_V_WIDTH = MLA_HEADS * MLA_V
MASK_VALUE = -1e30


def _rms(x, g):
    return x * lax.rsqrt(jnp.mean(x * x, axis=-1, keepdims=True) + NORM_EPS) * g


def _silu(x):
    return x * (1.0 / (1.0 + jnp.exp(-x)))


def _resident(shape):
    zeros = (0,) * len(shape)
    return pl.BlockSpec(shape, lambda *_: zeros, pipeline_mode=pl.Buffered(1))


def _params(*semantics):
    return pltpu.CompilerParams(dimension_semantics=semantics, vmem_limit_bytes=VMEM_LIMIT_BYTES)


def _row_tile(n, target):
    if n <= target:
        return n
    for t in range(target, 0, -16):
        if n % t == 0:
            return t
    raise ValueError((n, target))


INPROJ_COLS = 512


def _inproj_kernel(x_ref, g_ref, w_ref, wdt_ref, zx_ref, dt_ref):
    xn = _rms(x_ref[...], g_ref[...]).astype(BF16)
    for n in range(0, SSD_ZX_DIM, INPROJ_COLS):
        zx_ref[:, n:n + INPROJ_COLS] = jnp.dot(
            xn, w_ref[:, n:n + INPROJ_COLS], preferred_element_type=F32).astype(BF16)
    dt_ref[...] = jnp.dot(xn, wdt_ref[...], preferred_element_type=F32)


def _inproj(x, g, w_zx, w_dt):
    t = x.shape[0]
    tm = _row_tile(t, 512)
    return pl.pallas_call(
        _inproj_kernel,
        grid=(t // tm,),
        in_specs=[
            pl.BlockSpec((tm, D_MODEL), lambda i: (i, 0)),
            _resident((1, D_MODEL)),
            _resident((D_MODEL, SSD_ZX_DIM)),
            _resident((D_MODEL, LANES)),
        ],
        out_specs=[
            pl.BlockSpec((tm, SSD_ZX_DIM), lambda i: (i, 0)),
            pl.BlockSpec((tm, LANES), lambda i: (i, 0)),
        ],
        out_shape=[
            jax.ShapeDtypeStruct((t, SSD_ZX_DIM), BF16),
            jax.ShapeDtypeStruct((t, LANES), F32),
        ],
        compiler_params=_params("parallel"),
        name="ssd_inproj",
    )(x, g, w_zx, w_dt)


CONV_TAIL = SSD_CONV - 1
CONV_PAD = 8
CONV_COLS = 512
SSD_Q = LANES


def _ssd_kernel(zx_ref, dt_ref, h_ref, cinit_ref, sinit_ref, convw_ref, convb_ref, dtb_ref,
                alog_ref, dskip_ref, gn_ref, wout_ref, gpost_ref,
                hout_ref, cout_ref, sout_ref,
                xe_ref, xc_ref, st_ref, y_ref, act_ref, dtt_ref,
                *, tl, l_valid):
    t = pl.program_id(1)
    nt = pl.num_programs(1)
    q = SSD_Q
    gw = SSD_GROUP_WIDTH

    @pl.when(t == 0)
    def _():
        xe_ref[0:CONV_PAD, :] = jnp.zeros((CONV_PAD, SSD_CONV_DIM), F32)
        xe_ref[CONV_PAD - CONV_TAIL:CONV_PAD, :] = cinit_ref[...]
        for g in range(SSD_GROUPS):
            st_ref[g] = sinit_ref[g * gw:(g + 1) * gw, :].T

    @pl.when(t > 0)
    def _():
        xe_ref[0:CONV_PAD, :] = xe_ref[tl:tl + CONV_PAD, :]

    for c0 in range(0, SSD_CONV_DIM, CONV_COLS):
        cs = slice(c0, c0 + CONV_COLS)
        xe_ref[CONV_PAD:CONV_PAD + tl, cs] = (
            zx_ref[:, SSD_D_INNER + c0:SSD_D_INNER + c0 + CONV_COLS].astype(F32))
        acc = convb_ref[:, cs]
        for k in range(SSD_CONV):
            lo = CONV_PAD - CONV_TAIL + k
            acc = acc + xe_ref[lo:lo + tl, cs] * convw_ref[k:k + 1, cs]
        xc_ref[:, cs] = _silu(acc)

    a_row = -jnp.exp(alog_ref[...])
    rows = lax.broadcasted_iota(jnp.int32, (q, q), 0)
    cols = lax.broadcasted_iota(jnp.int32, (q, q), 1)
    causal = rows >= cols
    tri = causal.astype(F32)
    low_half = lax.broadcasted_iota(jnp.int32, (q, LANES), 1) < SSD_HEAD_DIM
    head_of_lane = lax.broadcasted_iota(jnp.int32, (q, gw), 1) // SSD_HEAD_DIM

    def per_head_lanes(parts):
        return jnp.concatenate([jnp.where(low_half, parts[0], parts[1]),
                                jnp.where(low_half, parts[2], parts[3])], axis=1)

    def chunk(c, carry):
        r0 = pl.multiple_of(c * q, q)
        rs = pl.ds(r0, q)
        raw = dt_ref[rs, :] + dtb_ref[...]
        dt = jnp.maximum(raw, 0.0) + jnp.log(1.0 + jnp.exp(-jnp.abs(raw)))
        if l_valid is not None:
            row_id = t * tl + r0 + lax.broadcasted_iota(jnp.int32, (q, LANES), 0)
            dt = jnp.where(row_id < l_valid, dt, 0.0)
        da = dt * a_row
        acum = jnp.dot(tri, da, preferred_element_type=F32, precision=lax.Precision.HIGHEST)
        act_ref[...] = acum.T
        dtt_ref[...] = dt.T

        for g in range(SSD_GROUPS):
            b0 = SSD_D_INNER + g * SSD_STATE
            c0 = SSD_D_INNER + SSD_GN + g * SSD_STATE
            gs = slice(g * gw, (g + 1) * gw)
            bg = xc_ref[rs, b0:b0 + SSD_STATE].astype(BF16)
            cg = xc_ref[rs, c0:c0 + SSD_STATE].astype(BF16)
            xg = xc_ref[rs, gs]
            xg_b = xg.astype(BF16)
            cb = lax.dot_general(cg, bg, (((1,), (1,)), ((), ())), preferred_element_type=F32)
            st = st_ref[g]
            y_state = jnp.dot(cg, st.astype(BF16), preferred_element_type=F32)

            acol_parts, dcol_parts, y_parts = [], [], []
            for r in range(SSD_HEADS_PER_GROUP):
                h = g * SSD_HEADS_PER_GROUP + r
                acol = jnp.broadcast_to(acum[:, h:h + 1], (q, LANES))
                dcol = jnp.broadcast_to(dt[:, h:h + 1], (q, LANES))
                arow = act_ref[h:h + 1, :]
                drow = dtt_ref[h:h + 1, :]
                seg = jnp.where(causal, acol - arow, MASK_VALUE)
                m = (cb * jnp.exp(seg) * drow).astype(BF16)
                y_parts.append(jnp.dot(m, xg_b, preferred_element_type=F32))
                acol_parts.append(acol)
                dcol_parts.append(dcol)

            acol_g = per_head_lanes(acol_parts)
            dcol_g = per_head_lanes(dcol_parts)
            y_in = jnp.where(head_of_lane == 0, y_parts[0],
                             jnp.where(head_of_lane == 1, y_parts[1],
                                       jnp.where(head_of_lane == 2, y_parts[2], y_parts[3])))
            y = y_in + jnp.exp(acol_g) * y_state + dskip_ref[:, gs] * xg

            a_last = acol_g[q - 1:q, :]
            w_end = jnp.exp(a_last - acol_g) * dcol_g
            xw = (xg * w_end).astype(BF16)
            upd = lax.dot_general(bg, xw, (((0,), (0,)), ((), ())), preferred_element_type=F32)
            st_ref[g] = jnp.exp(a_last) * st + upd

            zg = zx_ref[rs, gs].astype(F32)
            y_ref[rs, gs] = _rms(y * _silu(zg), gn_ref[:, gs]).astype(BF16)
        return carry

    lax.fori_loop(0, tl // q, chunk, 0)

    mix = jnp.dot(y_ref[...], wout_ref[...], preferred_element_type=F32)
    hout_ref[...] = h_ref[...] + _rms(mix, gpost_ref[...])

    @pl.when(t == nt - 1)
    def _():
        n_rows = tl if l_valid is None else (l_valid - 1) % tl + 1
        last = CONV_PAD + n_rows
        cout_ref[...] = xe_ref[last - CONV_TAIL:last, :]
        for g in range(SSD_GROUPS):
            sout_ref[g * gw:(g + 1) * gw, :] = st_ref[g].T


def _ssd(zx, dt_raw, h, conv_init, state_init, conv_w, conv_b, dt_bias, a_log, d_skip, gate_norm,
         w_out, g_post, *, l_valid):
    b, l, _ = zx.shape
    tl = _row_tile(l, 256)
    assert tl % SSD_Q == 0 and l - l_valid < tl
    hp = SSD_HEADS * SSD_HEAD_DIM
    kern = functools.partial(_ssd_kernel, tl=tl, l_valid=None if l_valid == l else l_valid)
    return pl.pallas_call(
        kern,
        grid=(b, l // tl),
        in_specs=[
            pl.BlockSpec((None, tl, SSD_ZX_DIM), lambda i, j: (i, j, 0)),
            pl.BlockSpec((None, tl, LANES), lambda i, j: (i, j, 0)),
            pl.BlockSpec((None, tl, D_MODEL), lambda i, j: (i, j, 0)),
            pl.BlockSpec((None, CONV_TAIL, SSD_CONV_DIM), lambda i, j: (i, 0, 0)),
            pl.BlockSpec((None, hp, SSD_STATE), lambda i, j: (i, 0, 0)),
            _resident((SSD_CONV, SSD_CONV_DIM)),
            _resident((1, SSD_CONV_DIM)),
            _resident((1, LANES)),
            _resident((1, LANES)),
            _resident((1, SSD_D_INNER)),
            _resident((1, SSD_D_INNER)),
            _resident((SSD_D_INNER, D_MODEL)),
            _resident((1, D_MODEL)),
        ],
        out_specs=[
            pl.BlockSpec((None, tl, D_MODEL), lambda i, j: (i, j, 0)),
            pl.BlockSpec((None, CONV_TAIL, SSD_CONV_DIM), lambda i, j: (i, 0, 0)),
            pl.BlockSpec((None, hp, SSD_STATE), lambda i, j: (i, 0, 0)),
        ],
        out_shape=[
            jax.ShapeDtypeStruct((b, l, D_MODEL), F32),
            jax.ShapeDtypeStruct((b, CONV_TAIL, SSD_CONV_DIM), F32),
            jax.ShapeDtypeStruct((b, hp, SSD_STATE), F32),
        ],
        scratch_shapes=[
            pltpu.VMEM((tl + CONV_PAD, SSD_CONV_DIM), F32),
            pltpu.VMEM((tl, SSD_CONV_DIM), F32),
            pltpu.VMEM((SSD_GROUPS, SSD_STATE, SSD_GROUP_WIDTH), F32),
            pltpu.VMEM((tl, SSD_D_INNER), BF16),
            pltpu.VMEM((LANES, SSD_Q), F32),
            pltpu.VMEM((LANES, SSD_Q), F32),
        ],
        compiler_params=_params("parallel", "arbitrary"),
        name="ssd_scan",
    )(zx, dt_raw, h, conv_init, state_init, conv_w, conv_b, dt_bias, a_log, d_skip, gate_norm,
      w_out, g_post)


FFN_COLS = MXU_WIDTH


def _ffn_kernel(h_ref, gpre_ref, wg_ref, wu_ref, wd_ref, gpost_ref, out_ref):
    h = h_ref[...]
    xn = _rms(h, gpre_ref[...]).astype(BF16)
    acc = None
    for f in range(0, FFN_HIDDEN, FFN_COLS):
        gate = jnp.dot(xn, wg_ref[:, f:f + FFN_COLS], preferred_element_type=F32)
        up = jnp.dot(xn, wu_ref[:, f:f + FFN_COLS], preferred_element_type=F32)
        act = (_silu(gate) * up).astype(BF16)
        part = jnp.dot(act, wd_ref[f:f + FFN_COLS, :], preferred_element_type=F32)
        acc = part if acc is None else acc + part
    out_ref[...] = h + _rms(acc, gpost_ref[...])


def _ffn(h, g_pre, w_gate, w_up, w_down, g_post):
    t = h.shape[0]
    tm = _row_tile(t, 512)
    return pl.pallas_call(
        _ffn_kernel,
        grid=(t // tm,),
        in_specs=[
            pl.BlockSpec((tm, D_MODEL), lambda i: (i, 0)),
            _resident((1, D_MODEL)),
            _resident((D_MODEL, FFN_HIDDEN)),
            _resident((D_MODEL, FFN_HIDDEN)),
            _resident((FFN_HIDDEN, D_MODEL)),
            _resident((1, D_MODEL)),
        ],
        out_specs=pl.BlockSpec((tm, D_MODEL), lambda i: (i, 0)),
        out_shape=jax.ShapeDtypeStruct((t, D_MODEL), F32),
        compiler_params=_params("parallel"),
        name="ffn",
    )(h, g_pre, w_gate, w_up, w_down, g_post)


def _rope_slot(x, cos, sin_hi, sin_lo):
    return (x * cos + pltpu.roll(x, MLA_ROPE_HALF, axis=1) * sin_hi
            + pltpu.roll(x, LANES - MLA_ROPE_HALF, axis=1) * sin_lo)


def _mla_pre_kernel(h_ref, gpre_ref, wqa_ref, qn_ref, wqb_ref, wlat_ref, kvn_ref, wkr_ref,
                    cos_ref, shi_ref, slo_ref, q_ref, lat_ref, kr_ref):
    xn = _rms(h_ref[...], gpre_ref[...]).astype(BF16)
    cos, s_hi, s_lo = cos_ref[...], shi_ref[...], slo_ref[...]

    qa = jnp.dot(xn, wqa_ref[...], preferred_element_type=F32)
    qa = _rms(qa, qn_ref[...]).astype(BF16)
    for hd in range(MLA_HEADS):
        sl = slice(hd * HEAD_SLOT, (hd + 1) * HEAD_SLOT)
        qh = jnp.dot(qa, wqb_ref[:, sl], preferred_element_type=F32)
        q_ref[:, sl] = (_rope_slot(qh, cos, s_hi, s_lo) * MLA_SCALE).astype(BF16)

    lat = jnp.dot(xn, wlat_ref[...], preferred_element_type=F32)
    lat_ref[...] = _rms(lat, kvn_ref[...])
    kr = _rope_slot(jnp.dot(xn, wkr_ref[...], preferred_element_type=F32), cos, s_hi, s_lo)
    kr_ref[...] = pltpu.roll(kr, LANES - ROPE_LO, axis=1)[:, :MLA_ROPE]


def _mla_pre(h, g_pre, wq_a, q_norm, wq_b, w_lat, kv_norm, w_kr, cos, s_hi, s_lo):
    b, l, _ = h.shape
    tm = _row_tile(l, 512)
    row = lambda i, j: (i, j, 0)
    tab = pl.BlockSpec((tm, LANES), lambda i, j: (j, 0))
    return pl.pallas_call(
        _mla_pre_kernel,
        grid=(b, l // tm),
        in_specs=[
            pl.BlockSpec((None, tm, D_MODEL), row),
            _resident((1, D_MODEL)),
            _resident((D_MODEL, MLA_Q_LORA)),
            _resident((1, MLA_Q_LORA)),
            _resident((MLA_Q_LORA, MLA_Q_WIDTH)),
            _resident((D_MODEL, MLA_KV_LORA)),
            _resident((1, MLA_KV_LORA)),
            _resident((D_MODEL, LANES)),
            tab, tab, tab,
        ],
        out_specs=[
            pl.BlockSpec((None, tm, MLA_Q_WIDTH), row),
            pl.BlockSpec((None, tm, MLA_KV_LORA), row),
            pl.BlockSpec((None, tm, MLA_ROPE), row),
        ],
        out_shape=[
            jax.ShapeDtypeStruct((b, l, MLA_Q_WIDTH), BF16),
            jax.ShapeDtypeStruct((b, l, MLA_KV_LORA), F32),
            jax.ShapeDtypeStruct((b, l, MLA_ROPE), F32),
        ],
        compiler_params=_params("parallel", "parallel"),
        name="mla_pre",
    )(h, g_pre, wq_a, q_norm, wq_b, w_lat, kv_norm, w_kr, cos, s_hi, s_lo)


def _kv_expand_kernel(lat_ref, kr_ref, wk_ref, wv_ref, place_ref, k_ref, v_ref):
    lat = lat_ref[...].astype(BF16)
    kr_slot = jnp.dot(kr_ref[...].astype(BF16), place_ref[...], preferred_element_type=F32)
    for hd in range(MLA_HEADS):
        sl = slice(hd * HEAD_SLOT, (hd + 1) * HEAD_SLOT)
        kh = jnp.dot(lat, wk_ref[:, sl], preferred_element_type=F32)
        k_ref[:, sl] = (kh + kr_slot).astype(BF16)
    v_ref[...] = jnp.dot(lat, wv_ref[...], preferred_element_type=F32).astype(BF16)


def _kv_expand(lat, kr, w_k, w_v, place):
    t = lat.shape[0]
    tm = _row_tile(t, 512)
    return pl.pallas_call(
        _kv_expand_kernel,
        grid=(t // tm,),
        in_specs=[
            pl.BlockSpec((tm, MLA_KV_LORA), lambda i: (i, 0)),
            pl.BlockSpec((tm, MLA_ROPE), lambda i: (i, 0)),
            _resident((MLA_KV_LORA, MLA_Q_WIDTH)),
            _resident((MLA_KV_LORA, MLA_V_WIDTH)),
            _resident((MLA_ROPE, LANES)),
        ],
        out_specs=[
            pl.BlockSpec((tm, MLA_Q_WIDTH), lambda i: (i, 0)),
            pl.BlockSpec((tm, MLA_V_WIDTH), lambda i: (i, 0)),
        ],
        out_shape=[
            jax.ShapeDtypeStruct((t, MLA_Q_WIDTH), BF16),
            jax.ShapeDtypeStruct((t, MLA_V_WIDTH), BF16),
        ],
        compiler_params=_params("parallel"),
        name="mla_kv_expand",
    )(lat, kr, w_k, w_v, place)


def _kv_tiles_needed(qi, *, tq, tk, q_off, lk):
    last_chunk = (q_off + qi * tq + tq - 1) // CHUNK
    k_end = jnp.minimum((last_chunk + 1) * CHUNK, lk)
    return (k_end + tk - 1) // tk


def _attn_kernel(q_ref, k_ref, v_ref, o_ref, m_ref, l_ref, acc_ref, *, tq, tk, q_off, lk):
    qi = pl.program_id(1)
    ki = pl.program_id(2)
    nk = pl.num_programs(2)

    @pl.when(ki == 0)
    def _():
        m_ref[...] = jnp.full(m_ref.shape, MASK_VALUE, F32)
        l_ref[...] = jnp.zeros(l_ref.shape, F32)
        acc_ref[...] = jnp.zeros(acc_ref.shape, F32)

    needed = _kv_tiles_needed(qi, tq=tq, tk=tk, q_off=q_off, lk=lk)
    first_chunk = (q_off + qi * tq) // CHUNK
    tile_last_chunk = (ki * tk + tk - 1) // CHUNK
    partial = jnp.logical_or(tile_last_chunk > first_chunk, ki * tk + tk > lk)

    def step(masked):
        if masked:
            q_chunk = (q_off + qi * tq + lax.broadcasted_iota(jnp.int32, (tq, tk), 0)) // CHUNK
            k_pos = ki * tk + lax.broadcasted_iota(jnp.int32, (tq, tk), 1)
            visible = jnp.logical_and(k_pos // CHUNK <= q_chunk, k_pos < lk)
        for hd in range(MLA_HEADS):
            sl = slice(hd * HEAD_SLOT, (hd + 1) * HEAD_SLOT)
            s = lax.dot_general(q_ref[:, sl], k_ref[:, sl], (((1,), (1,)), ((), ())),
                                preferred_element_type=F32)
            if masked:
                s = jnp.where(visible, s, MASK_VALUE)
            m_prev = m_ref[hd]
            m_new = jnp.maximum(m_prev, jnp.max(s, axis=-1, keepdims=True))
            alpha = jnp.exp(m_prev - m_new)
            p = jnp.exp(s - jnp.tile(m_new, (1, tk // LANES)))
            l_ref[hd] = alpha * l_ref[hd] + jnp.sum(p, axis=-1, keepdims=True)
            m_ref[hd] = m_new
            v_pair = v_ref[:, (hd // 2) * LANES:(hd // 2 + 1) * LANES]
            pv = jnp.dot(p.astype(BF16), v_pair, preferred_element_type=F32)
            acc_ref[hd] = alpha * acc_ref[hd] + pv

    @pl.when(jnp.logical_and(ki < needed, partial))
    def _():
        step(True)

    @pl.when(jnp.logical_and(ki < needed, jnp.logical_not(partial)))
    def _():
        step(False)

    @pl.when(ki == nk - 1)
    def _():
        low_half = lax.broadcasted_iota(jnp.int32, (tq, LANES), 1) < MLA_V
        for pair in range(MLA_HEADS // 2):
            even = acc_ref[2 * pair] / l_ref[2 * pair]
            odd = acc_ref[2 * pair + 1] / l_ref[2 * pair + 1]
            o_ref[:, pair * LANES:(pair + 1) * LANES] = jnp.where(low_half, even, odd).astype(BF16)


def _attention(q, k, v, *, q_off, lk):
    b, lq, _ = q.shape
    lk_pad = k.shape[1]
    tq = _row_tile(lq, 256)
    tk = lk_pad if lk_pad <= 1536 else 512
    assert lk_pad % tk == 0 and tk % LANES == 0
    static = dict(tq=tq, tk=tk, q_off=q_off, lk=lk)

    def kv_index(i, j, kk):
        return (i, jnp.minimum(kk, _kv_tiles_needed(j, **static) - 1), 0)

    return pl.pallas_call(
        functools.partial(_attn_kernel, **static),
        grid=(b, lq // tq, lk_pad // tk),
        in_specs=[
            pl.BlockSpec((None, tq, MLA_Q_WIDTH), lambda i, j, kk: (i, j, 0)),
            pl.BlockSpec((None, tk, MLA_Q_WIDTH), kv_index),
            pl.BlockSpec((None, tk, MLA_V_WIDTH), kv_index),
        ],
        out_specs=pl.BlockSpec((None, tq, MLA_V_WIDTH), lambda i, j, kk: (i, j, 0)),
        out_shape=jax.ShapeDtypeStruct((b, lq, MLA_V_WIDTH), BF16),
        scratch_shapes=[
            pltpu.VMEM((MLA_HEADS, tq, LANES), F32),
            pltpu.VMEM((MLA_HEADS, tq, LANES), F32),
            pltpu.VMEM((MLA_HEADS, tq, LANES), F32),
        ],
        compiler_params=_params("parallel", "parallel", "arbitrary"),
        name="mla_attention",
    )(q, k, v)


def _proj_res_kernel(a_ref, w_ref, g_ref, h_ref, out_ref):
    mix = jnp.dot(a_ref[...], w_ref[...], preferred_element_type=F32)
    out_ref[...] = h_ref[...] + _rms(mix, g_ref[...])


def _proj_res(a, w, g, h):
    t, k = a.shape
    tm = _row_tile(t, 512)
    return pl.pallas_call(
        _proj_res_kernel,
        grid=(t // tm,),
        in_specs=[
            pl.BlockSpec((tm, k), lambda i: (i, 0)),
            _resident((k, D_MODEL)),
            _resident((1, D_MODEL)),
            pl.BlockSpec((tm, D_MODEL), lambda i: (i, 0)),
        ],
        out_specs=pl.BlockSpec((tm, D_MODEL), lambda i: (i, 0)),
        out_shape=jax.ShapeDtypeStruct((t, D_MODEL), F32),
        compiler_params=_params("parallel"),
        name="proj_res",
    )(a, w, g, h)


def _prep_params(ln_mix_pre, ln_mix_post, ln_ffn_pre, ln_ffn_post,
                 ssd_w_in, ssd_conv_w, ssd_conv_b, ssd_dt_bias, ssd_a_log, ssd_d, ssd_gate_norm, ssd_w_out,
                 mla_wq_a, mla_q_norm, mla_wq_b, mla_wkv_a, mla_kv_norm, mla_w_uk, mla_w_uv, mla_w_o,
                 ffn_w_gate, ffn_w_up, ffn_w_down):
    row = lambda v: v.reshape(1, -1).astype(F32)
    pad_lanes = lambda v: jnp.pad(v, ((0, 0), (0, LANES - v.shape[1])))
    head_slots = lambda w, d: jnp.pad(
        w.reshape(w.shape[0], MLA_HEADS, d), ((0, 0), (0, 0), (0, HEAD_SLOT - d))).reshape(w.shape[0], -1)
    w_in = ssd_w_in[0]
    w_kr = jnp.zeros((D_MODEL, LANES), F32).at[:, ROPE_LO:ROPE_LO + MLA_ROPE].set(mla_wkv_a[0][:, MLA_KV_LORA:])
    place = jnp.zeros((MLA_ROPE, LANES), F32).at[jnp.arange(MLA_ROPE), ROPE_LO + jnp.arange(MLA_ROPE)].set(1.0)
    return dict(
        ln_mix_pre=[row(ln_mix_pre[i]) for i in range(2)],
        ln_mix_post=[row(ln_mix_post[i]) for i in range(2)],
        ln_ffn_pre=[row(ln_ffn_pre[i]) for i in range(2)],
        ln_ffn_post=[row(ln_ffn_post[i]) for i in range(2)],
        w_zx=w_in[:, :SSD_ZX_DIM].astype(BF16),
        w_dt=pad_lanes(w_in[:, SSD_ZX_DIM:]).astype(BF16),
        conv_w=ssd_conv_w[0].astype(F32),
        conv_b=row(ssd_conv_b[0]),
        dt_bias=pad_lanes(row(ssd_dt_bias[0])),
        a_log=pad_lanes(row(ssd_a_log[0])),
        d_skip=row(jnp.repeat(ssd_d[0], SSD_HEAD_DIM)),
        gate_norm=row(ssd_gate_norm[0]),
        w_out=ssd_w_out[0].astype(BF16),
        wq_a=mla_wq_a[0].astype(BF16),
        q_norm=row(mla_q_norm[0]),
        wq_b=head_slots(mla_wq_b[0], MLA_QK).astype(BF16),
        w_lat=mla_wkv_a[0][:, :MLA_KV_LORA].astype(BF16),
        kv_norm=row(mla_kv_norm[0]),
        w_kr=w_kr.astype(BF16),
        w_k=head_slots(mla_w_uk[0].reshape(MLA_KV_LORA, -1), MLA_NOPE).astype(BF16),
        w_v=mla_w_uv[0].reshape(MLA_KV_LORA, -1).astype(BF16),
        place=place.astype(BF16),
        w_o=mla_w_o[0].astype(BF16),
        ffn_w_gate=[ffn_w_gate[i].astype(BF16) for i in range(2)],
        ffn_w_up=[ffn_w_up[i].astype(BF16) for i in range(2)],
        ffn_w_down=[ffn_w_down[i].astype(BF16) for i in range(2)],
    )


def _rope_tables(pos):
    inv = ROPE_THETA ** (-jnp.arange(MLA_ROPE_HALF, dtype=F32) / MLA_ROPE_HALF)
    ang = pos.astype(F32)[:, None] * inv[None, :]
    cos, sin = jnp.cos(ang), jnp.sin(ang)
    l = pos.shape[0]
    ones = jnp.ones((l, ROPE_LO), F32)
    tail = jnp.ones((l, LANES - ROPE_LO - MLA_ROPE), F32)
    zeros_lo = jnp.zeros((l, ROPE_LO), F32)
    zeros_half = jnp.zeros((l, MLA_ROPE_HALF), F32)
    zeros_tail = jnp.zeros((l, LANES - ROPE_LO - MLA_ROPE), F32)
    cos_t = jnp.concatenate([ones, cos, cos, tail], axis=1)
    s_hi = jnp.concatenate([zeros_lo, zeros_half, sin, zeros_tail], axis=1)
    s_lo = jnp.concatenate([zeros_lo, -sin, zeros_half, zeros_tail], axis=1)
    return cos_t, s_hi, s_lo


def _round_up(n, m):
    return -(-n // m) * m


def _trunk(x, conv_init, state_init, past_lat, past_kr, p):
    b, l, _ = x.shape
    t = b * l
    past = past_lat.shape[1]
    flat = lambda a: a.reshape(t, a.shape[-1])

    zx, dt_raw = _inproj(flat(x), p['ln_mix_pre'][0], p['w_zx'], p['w_dt'])
    l_pad = _round_up(l, SSD_Q)
    seq = lambda a: jnp.pad(a.reshape(b, l, a.shape[-1]), ((0, 0), (0, l_pad - l), (0, 0)))
    h1, new_conv, new_state = _ssd(
        seq(zx), seq(dt_raw), seq(flat(x)), conv_init, state_init.reshape(b, -1, SSD_STATE),
        p['conv_w'], p['conv_b'], p['dt_bias'], p['a_log'], p['d_skip'], p['gate_norm'],
        p['w_out'], p['ln_mix_post'][0], l_valid=l)
    h1 = flat(h1[:, :l])
    h2 = _ffn(h1, p['ln_ffn_pre'][0], p['ffn_w_gate'][0], p['ffn_w_up'][0], p['ffn_w_down'][0],
              p['ln_ffn_post'][0])

    cos, s_hi, s_lo = _rope_tables(past + jnp.arange(l, dtype=jnp.int32))
    q, lat, kr = _mla_pre(h2.reshape(b, l, D_MODEL), p['ln_mix_pre'][1], p['wq_a'], p['q_norm'], p['wq_b'],
                          p['w_lat'], p['kv_norm'], p['w_kr'], cos, s_hi, s_lo)
    lk = past + l
    lk_pad = _round_up(lk, LANES)
    keys = lambda new, old: jnp.pad(jnp.concatenate([old.astype(F32), new], axis=1),
                                    ((0, 0), (0, lk_pad - lk), (0, 0)))
    k_all, v_all = _kv_expand(keys(lat, past_lat).reshape(b * lk_pad, -1),
                              keys(kr, past_kr).reshape(b * lk_pad, -1), p['w_k'], p['w_v'], p['place'])
    o = _attention(q, k_all.reshape(b, lk_pad, -1), v_all.reshape(b, lk_pad, -1), q_off=past, lk=lk)
    h3 = _proj_res(flat(o), p['w_o'], p['ln_mix_post'][1], h2)
    h4 = _ffn(h3, p['ln_ffn_pre'][1], p['ffn_w_gate'][1], p['ffn_w_up'][1], p['ffn_w_down'][1],
              p['ln_ffn_post'][1])

    return (h4.reshape(b, l, D_MODEL), new_conv[None],
            new_state.reshape(b, SSD_HEADS, SSD_HEAD_DIM, SSD_STATE)[None], lat[None], kr[None])


def kernel(x_prompt, x_sample, state_ssd_conv, state_ssd_ssm, cache_mla_latent, cache_mla_krope,
           ln_mix_pre, ln_mix_post, ln_ffn_pre, ln_ffn_post,
           ssd_w_in, ssd_conv_w, ssd_conv_b, ssd_dt_bias, ssd_a_log, ssd_d, ssd_gate_norm, ssd_w_out,
           mla_wq_a, mla_q_norm, mla_wq_b, mla_wkv_a, mla_kv_norm, mla_w_uk, mla_w_uv, mla_w_o,
           ffn_w_gate, ffn_w_up, ffn_w_down):
    p = _prep_params(ln_mix_pre, ln_mix_post, ln_ffn_pre, ln_ffn_post,
                     ssd_w_in, ssd_conv_w, ssd_conv_b, ssd_dt_bias, ssd_a_log, ssd_d, ssd_gate_norm,
                     ssd_w_out, mla_wq_a, mla_q_norm, mla_wq_b, mla_wkv_a, mla_kv_norm, mla_w_uk,
                     mla_w_uv, mla_w_o, ffn_w_gate, ffn_w_up, ffn_w_down)
    b = x_prompt.shape[0]
    y_p, p_conv, p_ssm, p_lat, p_kr = _trunk(
        x_prompt,
        jnp.zeros((b, CONV_TAIL, SSD_CONV_DIM), F32),
        jnp.zeros((b, SSD_HEADS, SSD_HEAD_DIM, SSD_STATE), F32),
        jnp.zeros((b, 0, MLA_KV_LORA), F32),
        jnp.zeros((b, 0, MLA_ROPE), F32), p)
    y_s, s_conv, s_ssm, s_lat, s_kr = _trunk(
        x_sample, state_ssd_conv[0], state_ssd_ssm[0], cache_mla_latent[0], cache_mla_krope[0], p)
    return (y_p, y_s, p_conv, p_ssm, p_lat, p_kr, s_conv, s_ssm, s_lat, s_kr)
```

```python
import functools
import math

import jax
import jax.numpy as jnp
from jax import lax
from jax.experimental import pallas as pl
from jax.experimental.pallas import tpu as pltpu

F32 = jnp.float32
BF16 = jnp.bfloat16

D_MODEL = 1024
NORM_EPS = 1e-6
CHUNK = 64

SSD_D_INNER = 2048
SSD_HEAD_DIM = 64
SSD_HEADS = 32
SSD_GROUPS = 8
SSD_HEADS_PER_GROUP = 4
SSD_STATE = 128
SSD_CONV = 4
SSD_GN = SSD_GROUPS * SSD_STATE
SSD_CONV_DIM = SSD_D_INNER + 2 * SSD_GN
SSD_ZX_DIM = SSD_D_INNER + SSD_CONV_DIM
SSD_GROUP_WIDTH = SSD_HEADS_PER_GROUP * SSD_HEAD_DIM

MLA_HEADS = 16
MLA_Q_LORA = 512
MLA_KV_LORA = 256
MLA_NOPE = 64
MLA_ROPE = 32
MLA_ROPE_HALF = MLA_ROPE // 2
MLA_QK = MLA_NOPE + MLA_ROPE
MLA_V = 64
MLA_SCALE = 1.0 / math.sqrt(MLA_QK)
MLA_SCALE_LOG2 = MLA_SCALE * math.log2(math.e)
ROPE_THETA = 10000.0

FFN_HIDDEN = 2816

LANES = 128
MXU_WIDTH = 256
VMEM_LIMIT_BYTES = 56 * 1024 * 1024

HEAD_SLOT = LANES
ROPE_LO = MLA_NOPE
ROPE_HI = MLA_NOPE + MLA_ROPE_HALF
MLA_Q_WIDTH = MLA_HEADS * HEAD_SLOT
MLA_V_WIDTH = MLA_HEADS * MLA_V
MASK_VALUE = -1e30


def _rms(x, g):
    return x * lax.rsqrt(jnp.mean(x * x, axis=-1, keepdims=True) + NORM_EPS) * g


def _silu(x):
    return x * (1.0 / (1.0 + jnp.exp(-x)))


def _resident(shape):
    zeros = (0,) * len(shape)
    return pl.BlockSpec(shape, lambda *_: zeros, pipeline_mode=pl.Buffered(1))


def _params(*semantics):
    return pltpu.CompilerParams(dimension_semantics=semantics, vmem_limit_bytes=VMEM_LIMIT_BYTES)


def _row_tile(n, target):
    if n <= target:
        return n
    for t in range(target, 0, -16):
        if n % t == 0:
            return t
    raise ValueError((n, target))


INPROJ_COLS = 512


def _inproj_kernel(x_ref, g_ref, w_ref, wdt_ref, zx_ref, dt_ref):
    xn = _rms(x_ref[...], g_ref[...]).astype(BF16)
    for n in range(0, SSD_ZX_DIM, INPROJ_COLS):
        zx_ref[:, n:n + INPROJ_COLS] = jnp.dot(
            xn, w_ref[:, n:n + INPROJ_COLS], preferred_element_type=F32).astype(BF16)
    dt_ref[...] = jnp.dot(xn, wdt_ref[...], preferred_element_type=F32)


def _inproj(x, g, w_zx, w_dt):
    t = x.shape[0]
    tm = _row_tile(t, 512)
    return pl.pallas_call(
        _inproj_kernel,
        grid=(t // tm,),
        in_specs=[
            pl.BlockSpec((tm, D_MODEL), lambda i: (i, 0)),
            _resident((1, D_MODEL)),
            _resident((D_MODEL, SSD_ZX_DIM)),
            _resident((D_MODEL, LANES)),
        ],
        out_specs=[
            pl.BlockSpec((tm, SSD_ZX_DIM), lambda i: (i, 0)),
            pl.BlockSpec((tm, LANES), lambda i: (i, 0)),
        ],
        out_shape=[
            jax.ShapeDtypeStruct((t, SSD_ZX_DIM), BF16),
            jax.ShapeDtypeStruct((t, LANES), F32),
        ],
        compiler_params=_params("parallel"),
        name="ssd_inproj",
    )(x, g, w_zx, w_dt)


CONV_TAIL = SSD_CONV - 1
CONV_PAD = 8
CONV_COLS = 512
SSD_Q = LANES


def _ssd_kernel(zx_ref, dt_ref, h_ref, cinit_ref, sinit_ref, convw_ref, convb_ref, dtb_ref,
                alog_ref, dskip_ref, gn_ref, wout_ref, gpost_ref,
                hout_ref, cout_ref, sout_ref,
                xe_ref, xc_ref, st_ref, y_ref, act_ref, dtt_ref,
                *, tl, l_valid):
    t = pl.program_id(1)
    nt = pl.num_programs(1)
    q = SSD_Q
    gw = SSD_GROUP_WIDTH

    @pl.when(t == 0)
    def _():
        xe_ref[0:CONV_PAD, :] = jnp.zeros((CONV_PAD, SSD_CONV_DIM), F32)
        xe_ref[CONV_PAD - CONV_TAIL:CONV_PAD, :] = cinit_ref[...]
        for g in range(SSD_GROUPS):
            st_ref[g] = sinit_ref[g * gw:(g + 1) * gw, :].T

    @pl.when(t > 0)
    def _():
        xe_ref[0:CONV_PAD, :] = xe_ref[tl:tl + CONV_PAD, :]

    for c0 in range(0, SSD_CONV_DIM, CONV_COLS):
        cs = slice(c0, c0 + CONV_COLS)
        xe_ref[CONV_PAD:CONV_PAD + tl, cs] = (
            zx_ref[:, SSD_D_INNER + c0:SSD_D_INNER + c0 + CONV_COLS].astype(F32))
        acc = convb_ref[:, cs]
        for k in range(SSD_CONV):
            lo = CONV_PAD - CONV_TAIL + k
            acc = acc + xe_ref[lo:lo + tl, cs] * convw_ref[k:k + 1, cs]
        xc_ref[:, cs] = _silu(acc)

    a_row = -jnp.exp(alog_ref[...])
    rows = lax.broadcasted_iota(jnp.int32, (q, q), 0)
    cols = lax.broadcasted_iota(jnp.int32, (q, q), 1)
    causal = rows >= cols
    tri = causal.astype(F32)
    low_half = lax.broadcasted_iota(jnp.int32, (q, LANES), 1) < SSD_HEAD_DIM
    head_of_lane = lax.broadcasted_iota(jnp.int32, (q, gw), 1) // SSD_HEAD_DIM

    def per_head_lanes(parts):
        return jnp.concatenate([jnp.where(low_half, parts[0], parts[1]),
                                jnp.where(low_half, parts[2], parts[3])], axis=1)

    def chunk(c, carry):
        r0 = pl.multiple_of(c * q, q)
        rs = pl.ds(r0, q)
        raw = dt_ref[rs, :] + dtb_ref[...]
        dt = jnp.maximum(raw, 0.0) + jnp.log(1.0 + jnp.exp(-jnp.abs(raw)))
        if l_valid is not None:
            row_id = t * tl + r0 + lax.broadcasted_iota(jnp.int32, (q, LANES), 0)
            dt = jnp.where(row_id < l_valid, dt, 0.0)
        da = dt * a_row
        acum = jnp.dot(tri, da, preferred_element_type=F32, precision=lax.Precision.HIGHEST)
        act_ref[...] = acum.T
        dtt_ref[...] = dt.T

        for g in range(SSD_GROUPS):
            b0 = SSD_D_INNER + g * SSD_STATE
            c0 = SSD_D_INNER + SSD_GN + g * SSD_STATE
            gs = slice(g * gw, (g + 1) * gw)
            bg = xc_ref[rs, b0:b0 + SSD_STATE].astype(BF16)
            cg = xc_ref[rs, c0:c0 + SSD_STATE].astype(BF16)
            xg = xc_ref[rs, gs]
            xg_b = xg.astype(BF16)
            cb = lax.dot_general(cg, bg, (((1,), (1,)), ((), ())), preferred_element_type=F32)
            st = st_ref[g]
            y_state = jnp.dot(cg, st.astype(BF16), preferred_element_type=F32)

            acol_parts, dcol_parts, y_parts = [], [], []
            for r in range(SSD_HEADS_PER_GROUP):
                h = g * SSD_HEADS_PER_GROUP + r
                acol = jnp.broadcast_to(acum[:, h:h + 1], (q, LANES))
                dcol = jnp.broadcast_to(dt[:, h:h + 1], (q, LANES))
                arow = act_ref[h:h + 1, :]
                drow = dtt_ref[h:h + 1, :]
                seg = jnp.where(causal, acol - arow, MASK_VALUE)
                m = (cb * jnp.exp(seg) * drow).astype(BF16)
                y_parts.append(jnp.dot(m, xg_b, preferred_element_type=F32))
                acol_parts.append(acol)
                dcol_parts.append(dcol)

            acol_g = per_head_lanes(acol_parts)
            dcol_g = per_head_lanes(dcol_parts)
            y_in = jnp.where(head_of_lane == 0, y_parts[0],
                             jnp.where(head_of_lane == 1, y_parts[1],
                                       jnp.where(head_of_lane == 2, y_parts[2], y_parts[3])))
            y = y_in + jnp.exp(acol_g) * y_state + dskip_ref[:, gs] * xg

            a_last = acol_g[q - 1:q, :]
            w_end = jnp.exp(a_last - acol_g) * dcol_g
            xw = (xg * w_end).astype(BF16)
            upd = lax.dot_general(bg, xw, (((0,), (0,)), ((), ())), preferred_element_type=F32)
            st_ref[g] = jnp.exp(a_last) * st + upd

            zg = zx_ref[rs, gs].astype(F32)
            y_ref[rs, gs] = _rms(y * _silu(zg), gn_ref[:, gs]).astype(BF16)
        return carry

    lax.fori_loop(0, tl // q, chunk, 0)

    mix = jnp.dot(y_ref[...], wout_ref[...], preferred_element_type=F32)
    hout_ref[...] = h_ref[...] + _rms(mix, gpost_ref[...])

    @pl.when(t == nt - 1)
    def _():
        n_rows = tl if l_valid is None else (l_valid - 1) % tl + 1
        last = CONV_PAD + n_rows
        cout_ref[...] = xe_ref[last - CONV_TAIL:last, :]
        for g in range(SSD_GROUPS):
            sout_ref[g * gw:(g + 1) * gw, :] = st_ref[g].T


def _ssd(zx, dt_raw, h, conv_init, state_init, conv_w, conv_b, dt_bias, a_log, d_skip, gate_norm,
         w_out, g_post, *, l_valid):
    b, l, _ = zx.shape
    tl = _row_tile(l, 256)
    assert tl % SSD_Q == 0 and l - l_valid < tl
    hp = SSD_HEADS * SSD_HEAD_DIM
    kern = functools.partial(_ssd_kernel, tl=tl, l_valid=None if l_valid == l else l_valid)
    return pl.pallas_call(
        kern,
        grid=(b, l // tl),
        in_specs=[
            pl.BlockSpec((None, tl, SSD_ZX_DIM), lambda i, j: (i, j, 0)),
            pl.BlockSpec((None, tl, LANES), lambda i, j: (i, j, 0)),
            pl.BlockSpec((None, tl, D_MODEL), lambda i, j: (i, j, 0)),
            pl.BlockSpec((None, CONV_TAIL, SSD_CONV_DIM), lambda i, j: (i, 0, 0)),
            pl.BlockSpec((None, hp, SSD_STATE), lambda i, j: (i, 0, 0)),
            _resident((SSD_CONV, SSD_CONV_DIM)),
            _resident((1, SSD_CONV_DIM)),
            _resident((1, LANES)),
            _resident((1, LANES)),
            _resident((1, SSD_D_INNER)),
            _resident((1, SSD_D_INNER)),
            _resident((SSD_D_INNER, D_MODEL)),
            _resident((1, D_MODEL)),
        ],
        out_specs=[
            pl.BlockSpec((None, tl, D_MODEL), lambda i, j: (i, j, 0)),
            pl.BlockSpec((None, CONV_TAIL, SSD_CONV_DIM), lambda i, j: (i, 0, 0)),
            pl.BlockSpec((None, hp, SSD_STATE), lambda i, j: (i, 0, 0)),
        ],
        out_shape=[
            jax.ShapeDtypeStruct((b, l, D_MODEL), F32),
            jax.ShapeDtypeStruct((b, CONV_TAIL, SSD_CONV_DIM), F32),
            jax.ShapeDtypeStruct((b, hp, SSD_STATE), F32),
        ],
        scratch_shapes=[
            pltpu.VMEM((tl + CONV_PAD, SSD_CONV_DIM), F32),
            pltpu.VMEM((tl, SSD_CONV_DIM), F32),
            pltpu.VMEM((SSD_GROUPS, SSD_STATE, SSD_GROUP_WIDTH), F32),
            pltpu.VMEM((tl, SSD_D_INNER), BF16),
            pltpu.VMEM((LANES, SSD_Q), F32),
            pltpu.VMEM((LANES, SSD_Q), F32),
        ],
        compiler_params=_params("parallel", "arbitrary"),
        name="ssd_scan",
    )(zx, dt_raw, h, conv_init, state_init, conv_w, conv_b, dt_bias, a_log, d_skip, gate_norm,
      w_out, g_post)


FFN_COLS = MXU_WIDTH


def _ffn_kernel(h_ref, gpre_ref, wg_ref, wu_ref, wd_ref, gpost_ref, out_ref):
    h = h_ref[...]
    xn = _rms(h, gpre_ref[...]).astype(BF16)
    acc = None
    for f in range(0, FFN_HIDDEN, FFN_COLS):
        gate = jnp.dot(xn, wg_ref[:, f:f + FFN_COLS], preferred_element_type=F32)
        up = jnp.dot(xn, wu_ref[:, f:f + FFN_COLS], preferred_element_type=F32)
        act = (_silu(gate) * up).astype(BF16)
        part = jnp.dot(act, wd_ref[f:f + FFN_COLS, :], preferred_element_type=F32)
        acc = part if acc is None else acc + part
    out_ref[...] = h + _rms(acc, gpost_ref[...])


def _ffn(h, g_pre, w_gate, w_up, w_down, g_post):
    t = h.shape[0]
    tm = _row_tile(t, 512)
    return pl.pallas_call(
        _ffn_kernel,
        grid=(t // tm,),
        in_specs=[
            pl.BlockSpec((tm, D_MODEL), lambda i: (i, 0)),
            _resident((1, D_MODEL)),
            _resident((D_MODEL, FFN_HIDDEN)),
            _resident((D_MODEL, FFN_HIDDEN)),
            _resident((FFN_HIDDEN, D_MODEL)),
            _resident((1, D_MODEL)),
        ],
        out_specs=pl.BlockSpec((tm, D_MODEL), lambda i: (i, 0)),
        out_shape=jax.ShapeDtypeStruct((t, D_MODEL), F32),
        compiler_params=_params("parallel"),
        name="ffn",
    )(h, g_pre, w_gate, w_up, w_down, g_post)


def _rope_slot(x, cos, sin_hi, sin_lo):
    return (x * cos + pltpu.roll(x, MLA_ROPE_HALF, axis=1) * sin_hi
            + pltpu.roll(x, LANES - MLA_ROPE_HALF, axis=1) * sin_lo)


def _mla_pre_kernel(h_ref, gpre_ref, wqa_ref, qn_ref, wqb_ref, wlat_ref, kvn_ref, wkr_ref,
                    cos_ref, shi_ref, slo_ref, q_ref, lat_ref, kr_ref):
    xn = _rms(h_ref[...], gpre_ref[...]).astype(BF16)
    cos, s_hi, s_lo = cos_ref[...], shi_ref[...], slo_ref[...]

    qa = jnp.dot(xn, wqa_ref[...], preferred_element_type=F32)
    qa = _rms(qa, qn_ref[...]).astype(BF16)
    for hd in range(MLA_HEADS):
        sl = slice(hd * HEAD_SLOT, (hd + 1) * HEAD_SLOT)
        qh = jnp.dot(qa, wqb_ref[:, sl], preferred_element_type=F32)
        q_ref[:, sl] = (_rope_slot(qh, cos, s_hi, s_lo) * MLA_SCALE_LOG2).astype(BF16)

    lat = jnp.dot(xn, wlat_ref[...], preferred_element_type=F32)
    lat_ref[...] = _rms(lat, kvn_ref[...])
    kr = _rope_slot(jnp.dot(xn, wkr_ref[...], preferred_element_type=F32), cos, s_hi, s_lo)
    kr_ref[...] = pltpu.roll(kr, LANES - ROPE_LO, axis=1)[:, :MLA_ROPE]


def _mla_pre(h, g_pre, wq_a, q_norm, wq_b, w_lat, kv_norm, w_kr, cos, s_hi, s_lo):
    b, l, _ = h.shape
    tm = _row_tile(l, 512)
    row = lambda i, j: (i, j, 0)
    tab = pl.BlockSpec((tm, LANES), lambda i, j: (j, 0))
    return pl.pallas_call(
        _mla_pre_kernel,
        grid=(b, l // tm),
        in_specs=[
            pl.BlockSpec((None, tm, D_MODEL), row),
            _resident((1, D_MODEL)),
            _resident((D_MODEL, MLA_Q_LORA)),
            _resident((1, MLA_Q_LORA)),
            _resident((MLA_Q_LORA, MLA_Q_WIDTH)),
            _resident((D_MODEL, MLA_KV_LORA)),
            _resident((1, MLA_KV_LORA)),
            _resident((D_MODEL, LANES)),
            tab, tab, tab,
        ],
        out_specs=[
            pl.BlockSpec((None, tm, MLA_Q_WIDTH), row),
            pl.BlockSpec((None, tm, MLA_KV_LORA), row),
            pl.BlockSpec((None, tm, MLA_ROPE), row),
        ],
        out_shape=[
            jax.ShapeDtypeStruct((b, l, MLA_Q_WIDTH), BF16),
            jax.ShapeDtypeStruct((b, l, MLA_KV_LORA), F32),
            jax.ShapeDtypeStruct((b, l, MLA_ROPE), F32),
        ],
        compiler_params=_params("parallel", "parallel"),
        name="mla_pre",
    )(h, g_pre, wq_a, q_norm, wq_b, w_lat, kv_norm, w_kr, cos, s_hi, s_lo)


def _kv_expand_kernel(lat_ref, kr_ref, wk_ref, wv_ref, place_ref, k_ref, v_ref):
    lat = lat_ref[...].astype(BF16)
    kr_slot = jnp.dot(kr_ref[...].astype(BF16), place_ref[...], preferred_element_type=F32)
    ones_lane = (lax.broadcasted_iota(jnp.int32, (1, HEAD_SLOT), 1) == MLA_V).astype(F32)
    for hd in range(MLA_HEADS):
        sl = slice(hd * HEAD_SLOT, (hd + 1) * HEAD_SLOT)
        kh = jnp.dot(lat, wk_ref[:, sl], preferred_element_type=F32)
        k_ref[:, sl] = (kh + kr_slot).astype(BF16)
        vh = jnp.dot(lat, wv_ref[:, sl], preferred_element_type=F32)
        v_ref[:, sl] = (vh + ones_lane).astype(BF16)


def _kv_expand(lat, kr, w_k, w_v, place):
    t = lat.shape[0]
    tm = _row_tile(t, 512)
    return pl.pallas_call(
        _kv_expand_kernel,
        grid=(t // tm,),
        in_specs=[
            pl.BlockSpec((tm, MLA_KV_LORA), lambda i: (i, 0)),
            pl.BlockSpec((tm, MLA_ROPE), lambda i: (i, 0)),
            _resident((MLA_KV_LORA, MLA_Q_WIDTH)),
            _resident((MLA_KV_LORA, MLA_Q_WIDTH)),
            _resident((MLA_ROPE, LANES)),
        ],
        out_specs=[
            pl.BlockSpec((tm, MLA_Q_WIDTH), lambda i: (i, 0)),
            pl.BlockSpec((tm, MLA_Q_WIDTH), lambda i: (i, 0)),
        ],
        out_shape=[
            jax.ShapeDtypeStruct((t, MLA_Q_WIDTH), BF16),
            jax.ShapeDtypeStruct((t, MLA_Q_WIDTH), BF16),
        ],
        compiler_params=_params("parallel"),
        name="mla_kv_expand",
    )(lat, kr, w_k, w_v, place)


ATTN_SUB_K = 256


def _kv_tiles_needed(qi, *, tq, tk, q_off, lk):
    last_chunk = (q_off + qi * tq + tq - 1) // CHUNK
    k_end = min((last_chunk + 1) * CHUNK, lk)
    return (k_end + tk - 1) // tk


def _attn_kernel(qi_ref, ki_ref, last_ref, q_ref, k_ref, v_ref, o_ref, m_ref, acc_ref,
                 *, tq, tk, q_off, lk):
    step_id = pl.program_id(1)
    qi = qi_ref[step_id]
    ki = ki_ref[step_id]

    @pl.when(ki == 0)
    def _():
        m_ref[...] = jnp.full(m_ref.shape, MASK_VALUE, F32)
        acc_ref[...] = jnp.zeros(acc_ref.shape, F32)

    first_chunk = (q_off + qi * tq) // CHUNK
    tile_last_chunk = (ki * tk + tk - 1) // CHUNK
    partial = jnp.logical_or(tile_last_chunk > first_chunk, ki * tk + tk > lk)

    def step(masked):
        if masked:
            q_chunk = (q_off + qi * tq + lax.broadcasted_iota(jnp.int32, (tq, tk), 0)) // CHUNK
            k_pos = ki * tk + lax.broadcasted_iota(jnp.int32, (tq, tk), 1)
            visible = jnp.logical_and(k_pos // CHUNK <= q_chunk, k_pos < lk)
        for hd in range(MLA_HEADS):
            sl = slice(hd * HEAD_SLOT, (hd + 1) * HEAD_SLOT)
            for k0 in range(0, tk, ATTN_SUB_K):
                ks = slice(k0, k0 + ATTN_SUB_K)
                s = lax.dot_general(q_ref[:, sl], k_ref[ks, sl], (((1,), (1,)), ((), ())),
                                    preferred_element_type=F32)
                if masked:
                    s = jnp.where(visible[:, ks], s, MASK_VALUE)
                m_prev = m_ref[hd]
                m_new = jnp.maximum(m_prev, jnp.max(s, axis=-1, keepdims=True))
                alpha = jnp.exp2(m_prev - m_new)
                p = jnp.exp2(s - jnp.tile(m_new, (1, ATTN_SUB_K // LANES)))
                m_ref[hd] = m_new
                pv = jnp.dot(p.astype(BF16), v_ref[ks, sl], preferred_element_type=F32)
                acc_ref[hd] = alpha * acc_ref[hd] + pv

    @pl.when(partial)
    def _():
        step(True)

    @pl.when(jnp.logical_not(partial))
    def _():
        step(False)

    @pl.when(last_ref[step_id] == 1)
    def _():
        low_half = lax.broadcasted_iota(jnp.int32, (tq, LANES), 1) < MLA_V
        for pair in range(MLA_HEADS // 2):
            even = acc_ref[2 * pair]
            odd = acc_ref[2 * pair + 1]
            even = even / even[:, MLA_V:MLA_V + 1]
            odd = odd / odd[:, MLA_V:MLA_V + 1]
            packed = jnp.where(low_half, even, pltpu.roll(odd, MLA_V, axis=1))
            o_ref[:, pair * LANES:(pair + 1) * LANES] = packed.astype(BF16)


def _attention(q, k, v, *, q_off, lk):
    b, lq, _ = q.shape
    lk_pad = k.shape[1]
    tq = _row_tile(lq, 256)
    tk = lk_pad if lk_pad <= 1536 else 512
    assert lk_pad % tk == 0 and tk % LANES == 0
    static = dict(tq=tq, tk=tk, q_off=q_off, lk=lk)

    qi_tab, ki_tab, last_tab = [], [], []
    for qi in range(lq // tq):
        needed = _kv_tiles_needed(qi, **static)
        qi_tab += [qi] * needed
        ki_tab += list(range(needed))
        last_tab += [0] * (needed - 1) + [1]
    tables = [jnp.asarray(tab, jnp.int32) for tab in (qi_tab, ki_tab, last_tab)]

    q_index = lambda i, s, qi_ref, ki_ref, last_ref: (i, qi_ref[s], 0)
    k_index = lambda i, s, qi_ref, ki_ref, last_ref: (i, ki_ref[s], 0)
    return pl.pallas_call(
        functools.partial(_attn_kernel, **static),
        grid_spec=pltpu.PrefetchScalarGridSpec(
            num_scalar_prefetch=3,
            grid=(b, len(qi_tab)),
            in_specs=[
                pl.BlockSpec((None, tq, MLA_Q_WIDTH), q_index),
                pl.BlockSpec((None, tk, MLA_Q_WIDTH), k_index),
                pl.BlockSpec((None, tk, MLA_Q_WIDTH), k_index),
            ],
            out_specs=pl.BlockSpec((None, tq, MLA_V_WIDTH), q_index),
            scratch_shapes=[
                pltpu.VMEM((MLA_HEADS, tq, LANES), F32),
                pltpu.VMEM((MLA_HEADS, tq, LANES), F32),
            ],
        ),
        out_shape=jax.ShapeDtypeStruct((b, lq, MLA_V_WIDTH), BF16),
        compiler_params=_params("parallel", "arbitrary"),
        name="mla_attention",
    )(*tables, q, k, v)


def _proj_res_kernel(a_ref, w_ref, g_ref, h_ref, out_ref):
    mix = jnp.dot(a_ref[...], w_ref[...], preferred_element_type=F32)
    out_ref[...] = h_ref[...] + _rms(mix, g_ref[...])


def _proj_res(a, w, g, h):
    t, k = a.shape
    tm = _row_tile(t, 512)
    return pl.pallas_call(
        _proj_res_kernel,
        grid=(t // tm,),
        in_specs=[
            pl.BlockSpec((tm, k), lambda i: (i, 0)),
            _resident((k, D_MODEL)),
            _resident((1, D_MODEL)),
            pl.BlockSpec((tm, D_MODEL), lambda i: (i, 0)),
        ],
        out_specs=pl.BlockSpec((tm, D_MODEL), lambda i: (i, 0)),
        out_shape=jax.ShapeDtypeStruct((t, D_MODEL), F32),
        compiler_params=_params("parallel"),
        name="proj_res",
    )(a, w, g, h)


def _prep_params(ln_mix_pre, ln_mix_post, ln_ffn_pre, ln_ffn_post,
                 ssd_w_in, ssd_conv_w, ssd_conv_b, ssd_dt_bias, ssd_a_log, ssd_d, ssd_gate_norm, ssd_w_out,
                 mla_wq_a, mla_q_norm, mla_wq_b, mla_wkv_a, mla_kv_norm, mla_w_uk, mla_w_uv, mla_w_o,
                 ffn_w_gate, ffn_w_up, ffn_w_down):
    row = lambda v: v.reshape(1, -1).astype(F32)
    pad_lanes = lambda v: jnp.pad(v, ((0, 0), (0, LANES - v.shape[1])))
    head_slots = lambda w, d: jnp.pad(
        w.reshape(w.shape[0], MLA_HEADS, d), ((0, 0), (0, 0), (0, HEAD_SLOT - d))).reshape(w.shape[0], -1)
    w_in = ssd_w_in[0]
    w_kr = jnp.zeros((D_MODEL, LANES), F32).at[:, ROPE_LO:ROPE_LO + MLA_ROPE].set(mla_wkv_a[0][:, MLA_KV_LORA:])
    place = jnp.zeros((MLA_ROPE, LANES), F32).at[jnp.arange(MLA_ROPE), ROPE_LO + jnp.arange(MLA_ROPE)].set(1.0)
    return dict(
        ln_mix_pre=[row(ln_mix_pre[i]) for i in range(2)],
        ln_mix_post=[row(ln_mix_post[i]) for i in range(2)],
        ln_ffn_pre=[row(ln_ffn_pre[i]) for i in range(2)],
        ln_ffn_post=[row(ln_ffn_post[i]) for i in range(2)],
        w_zx=w_in[:, :SSD_ZX_DIM].astype(BF16),
        w_dt=pad_lanes(w_in[:, SSD_ZX_DIM:]).astype(BF16),
        conv_w=ssd_conv_w[0].astype(F32),
        conv_b=row(ssd_conv_b[0]),
        dt_bias=pad_lanes(row(ssd_dt_bias[0])),
        a_log=pad_lanes(row(ssd_a_log[0])),
        d_skip=row(jnp.repeat(ssd_d[0], SSD_HEAD_DIM)),
        gate_norm=row(ssd_gate_norm[0]),
        w_out=ssd_w_out[0].astype(BF16),
        wq_a=mla_wq_a[0].astype(BF16),
        q_norm=row(mla_q_norm[0]),
        wq_b=head_slots(mla_wq_b[0], MLA_QK).astype(BF16),
        w_lat=mla_wkv_a[0][:, :MLA_KV_LORA].astype(BF16),
        kv_norm=row(mla_kv_norm[0]),
        w_kr=w_kr.astype(BF16),
        w_k=head_slots(mla_w_uk[0].reshape(MLA_KV_LORA, -1), MLA_NOPE).astype(BF16),
        w_v=head_slots(mla_w_uv[0].reshape(MLA_KV_LORA, -1), MLA_V).astype(BF16),
        place=place.astype(BF16),
        w_o=mla_w_o[0].astype(BF16),
        ffn_w_gate=[ffn_w_gate[i].astype(BF16) for i in range(2)],
        ffn_w_up=[ffn_w_up[i].astype(BF16) for i in range(2)],
        ffn_w_down=[ffn_w_down[i].astype(BF16) for i in range(2)],
    )


def _rope_tables(pos):
    inv = ROPE_THETA ** (-jnp.arange(MLA_ROPE_HALF, dtype=F32) / MLA_ROPE_HALF)
    ang = pos.astype(F32)[:, None] * inv[None, :]
    cos, sin = jnp.cos(ang), jnp.sin(ang)
    l = pos.shape[0]
    ones = jnp.ones((l, ROPE_LO), F32)
    tail = jnp.ones((l, LANES - ROPE_LO - MLA_ROPE), F32)
    zeros_lo = jnp.zeros((l, ROPE_LO), F32)
    zeros_half = jnp.zeros((l, MLA_ROPE_HALF), F32)
    zeros_tail = jnp.zeros((l, LANES - ROPE_LO - MLA_ROPE), F32)
    cos_t = jnp.concatenate([ones, cos, cos, tail], axis=1)
    s_hi = jnp.concatenate([zeros_lo, zeros_half, sin, zeros_tail], axis=1)
    s_lo = jnp.concatenate([zeros_lo, -sin, zeros_half, zeros_tail], axis=1)
    return cos_t, s_hi, s_lo


def _round_up(n, m):
    return -(-n // m) * m


def _trunk(x, conv_init, state_init, past_lat, past_kr, p):
    b, l, _ = x.shape
    t = b * l
    past = past_lat.shape[1]
    flat = lambda a: a.reshape(t, a.shape[-1])

    zx, dt_raw = _inproj(flat(x), p['ln_mix_pre'][0], p['w_zx'], p['w_dt'])
    l_pad = _round_up(l, SSD_Q)
    seq = lambda a: jnp.pad(a.reshape(b, l, a.shape[-1]), ((0, 0), (0, l_pad - l), (0, 0)))
    h1, new_conv, new_state = _ssd(
        seq(zx), seq(dt_raw), seq(flat(x)), conv_init, state_init.reshape(b, -1, SSD_STATE),
        p['conv_w'], p['conv_b'], p['dt_bias'], p['a_log'], p['d_skip'], p['gate_norm'],
        p['w_out'], p['ln_mix_post'][0], l_valid=l)
    h1 = flat(h1[:, :l])
    h2 = _ffn(h1, p['ln_ffn_pre'][0], p['ffn_w_gate'][0], p['ffn_w_up'][0], p['ffn_w_down'][0],
              p['ln_ffn_post'][0])

    cos, s_hi, s_lo = _rope_tables(past + jnp.arange(l, dtype=jnp.int32))
    q, lat, kr = _mla_pre(h2.reshape(b, l, D_MODEL), p['ln_mix_pre'][1], p['wq_a'], p['q_norm'], p['wq_b'],
                          p['w_lat'], p['kv_norm'], p['w_kr'], cos, s_hi, s_lo)
    lk = past + l
    lk_pad = _round_up(lk, ATTN_SUB_K)
    keys = lambda new, old: jnp.pad(jnp.concatenate([old.astype(F32), new], axis=1),
                                    ((0, 0), (0, lk_pad - lk), (0, 0)))
    k_all, v_all = _kv_expand(keys(lat, past_lat).reshape(b * lk_pad, -1),
                              keys(kr, past_kr).reshape(b * lk_pad, -1), p['w_k'], p['w_v'], p['place'])
    o = _attention(q, k_all.reshape(b, lk_pad, -1), v_all.reshape(b, lk_pad, -1), q_off=past, lk=lk)
    h3 = _proj_res(flat(o), p['w_o'], p['ln_mix_post'][1], h2)
    h4 = _ffn(h3, p['ln_ffn_pre'][1], p['ffn_w_gate'][1], p['ffn_w_up'][1], p['ffn_w_down'][1],
              p['ln_ffn_post'][1])

    return (h4.reshape(b, l, D_MODEL), new_conv[None],
            new_state.reshape(b, SSD_HEADS, SSD_HEAD_DIM, SSD_STATE)[None], lat[None], kr[None])


def kernel(x_prompt, x_sample, state_ssd_conv, state_ssd_ssm, cache_mla_latent, cache_mla_krope,
           ln_mix_pre, ln_mix_post, ln_ffn_pre, ln_ffn_post,
           ssd_w_in, ssd_conv_w, ssd_conv_b, ssd_dt_bias, ssd_a_log, ssd_d, ssd_gate_norm, ssd_w_out,
           mla_wq_a, mla_q_norm, mla_wq_b, mla_wkv_a, mla_kv_norm, mla_w_uk, mla_w_uv, mla_w_o,
           ffn_w_gate, ffn_w_up, ffn_w_down):
    p = _prep_params(ln_mix_pre, ln_mix_post, ln_ffn_pre, ln_ffn_post,
                     ssd_w_in, ssd_conv_w, ssd_conv_b, ssd_dt_bias, ssd_a_log, ssd_d, ssd_gate_norm,
                     ssd_w_out, mla_wq_a, mla_q_norm, mla_wq_b, mla_wkv_a, mla_kv_norm, mla_w_uk,
                     mla_w_uv, mla_w_o, ffn_w_gate, ffn_w_up, ffn_w_down)
    b = x_prompt.shape[0]
    y_p, p_conv, p_ssm, p_lat, p_kr = _trunk(
        x_prompt,
        jnp.zeros((b, CONV_TAIL, SSD_CONV_DIM), F32),
        jnp.zeros((b, SSD_HEADS, SSD_HEAD_DIM, SSD_STATE), F32),
        jnp.zeros((b, 0, MLA_KV_LORA), F32),
        jnp.zeros((b, 0, MLA_ROPE), F32), p)
    y_s, s_conv, s_ssm, s_lat, s_kr = _trunk(
        x_sample, state_ssd_conv[0], state_ssd_ssm[0], cache_mla_latent[0], cache_mla_krope[0], p)
    return (y_p, y_s, p_conv, p_ssm, p_lat, p_kr, s_conv, s_ssm, s_lat, s_kr)
```

```python
import functools
import math

import jax
import jax.numpy as jnp
from jax import lax
from jax.experimental import pallas as pl
from jax.experimental.pallas import tpu as pltpu

F32 = jnp.float32
BF16 = jnp.bfloat16

D_MODEL = 1024
NORM_EPS = 1e-6
CHUNK = 64

SSD_D_INNER = 2048
SSD_HEAD_DIM = 64
SSD_HEADS = 32
SSD_GROUPS = 8
SSD_HEADS_PER_GROUP = 4
SSD_STATE = 128
SSD_CONV = 4
SSD_GN = SSD_GROUPS * SSD_STATE
SSD_CONV_DIM = SSD_D_INNER + 2 * SSD_GN
SSD_ZX_DIM = SSD_D_INNER + SSD_CONV_DIM
SSD_GROUP_WIDTH = SSD_HEADS_PER_GROUP * SSD_HEAD_DIM

MLA_HEADS = 16
MLA_Q_LORA = 512
MLA_KV_LORA = 256
MLA_NOPE = 64
MLA_ROPE = 32
MLA_ROPE_HALF = MLA_ROPE // 2
MLA_QK = MLA_NOPE + MLA_ROPE
MLA_V = 64
MLA_SCALE = 1.0 / math.sqrt(MLA_QK)
MLA_SCALE_LOG2 = MLA_SCALE * math.log2(math.e)
ROPE_THETA = 10000.0

FFN_HIDDEN = 2816

LANES = 128
MXU_WIDTH = 256
VMEM_LIMIT_BYTES = 56 * 1024 * 1024

HEAD_SLOT = LANES
ROPE_LO = MLA_NOPE
ROPE_HI = MLA_NOPE + MLA_ROPE_HALF
MLA_Q_WIDTH = MLA_HEADS * HEAD_SLOT
MLA_V_WIDTH = MLA_HEADS * MLA_V
MASK_VALUE = -1e30


def _rms(x, g):
    return x * lax.rsqrt(jnp.mean(x * x, axis=-1, keepdims=True) + NORM_EPS) * g


def _silu(x):
    half = 0.5 * x
    return half + half * jnp.tanh(half)


def _resident(shape):
    zeros = (0,) * len(shape)
    return pl.BlockSpec(shape, lambda *_: zeros, pipeline_mode=pl.Buffered(1))


def _params(*semantics):
    return pltpu.CompilerParams(dimension_semantics=semantics, vmem_limit_bytes=VMEM_LIMIT_BYTES)


def _row_tile(n, target):
    if n <= target:
        return n
    for t in range(target, 0, -16):
        if n % t == 0:
            return t
    raise ValueError((n, target))


INPROJ_COLS = 512


def _inproj_kernel(x_ref, g_ref, w_ref, wdt_ref, zx_ref, dt_ref):
    xn = _rms(x_ref[...], g_ref[...]).astype(BF16)
    for n in range(0, SSD_ZX_DIM, INPROJ_COLS):
        zx_ref[:, n:n + INPROJ_COLS] = jnp.dot(
            xn, w_ref[:, n:n + INPROJ_COLS], preferred_element_type=F32).astype(BF16)
    dt_ref[...] = jnp.dot(xn, wdt_ref[...], preferred_element_type=F32)


def _inproj(x, g, w_zx, w_dt):
    t = x.shape[0]
    tm = _row_tile(t, 512)
    return pl.pallas_call(
        _inproj_kernel,
        grid=(t // tm,),
        in_specs=[
            pl.BlockSpec((tm, D_MODEL), lambda i: (i, 0)),
            _resident((1, D_MODEL)),
            _resident((D_MODEL, SSD_ZX_DIM)),
            _resident((D_MODEL, LANES)),
        ],
        out_specs=[
            pl.BlockSpec((tm, SSD_ZX_DIM), lambda i: (i, 0)),
            pl.BlockSpec((tm, LANES), lambda i: (i, 0)),
        ],
        out_shape=[
            jax.ShapeDtypeStruct((t, SSD_ZX_DIM), BF16),
            jax.ShapeDtypeStruct((t, LANES), F32),
        ],
        compiler_params=_params("parallel"),
        name="ssd_inproj",
    )(x, g, w_zx, w_dt)


CONV_TAIL = SSD_CONV - 1
CONV_PAD = 8
CONV_GROUP = 16
CONV_COLS = 512
SSD_Q = LANES


def _ssd_kernel(zx_ref, dt_ref, h_ref, cinit_ref, sinit_ref, convw_ref, convb_ref, dtb_ref,
                alog_ref, dskip_ref, gn_ref, wout_ref, gpost_ref,
                hout_ref, cout_ref, sout_ref,
                tail_ref, head_ref, xc_ref, st_ref, y_ref, act_ref,
                *, tl, l_valid):
    t = pl.program_id(1)
    nt = pl.num_programs(1)
    q = SSD_Q
    gw = SSD_GROUP_WIDTH

    @pl.when(t == 0)
    def _():
        tail_ref[...] = jnp.zeros((CONV_PAD, SSD_CONV_DIM), F32)
        tail_ref[CONV_PAD - CONV_TAIL:CONV_PAD, :] = cinit_ref[...]
        for g in range(SSD_GROUPS):
            st_ref[g] = sinit_ref[g * gw:(g + 1) * gw, :].T

    srow = lax.broadcasted_iota(jnp.int32, (tl, tl), 0)
    scol = lax.broadcasted_iota(jnp.int32, (tl, tl), 1)
    shifts = [(srow - scol == j).astype(BF16) for j in range(1, SSD_CONV)]
    for c0 in range(0, SSD_CONV_DIM, CONV_COLS):
        cs = slice(c0, c0 + CONV_COLS)
        xb = zx_ref[:, SSD_D_INNER + c0:SSD_D_INNER + c0 + CONV_COLS]
        x = xb.astype(F32)
        acc = convb_ref[:, cs] + x * convw_ref[SSD_CONV - 1:SSD_CONV, cs]
        for j in range(1, SSD_CONV):
            shifted = jnp.dot(shifts[j - 1], xb, preferred_element_type=F32)
            acc = acc + shifted * convw_ref[SSD_CONV - 1 - j:SSD_CONV - j, cs]
        xc_ref[:, cs] = _silu(acc)

        head_ref[0:CONV_PAD, :] = tail_ref[:, cs]
        head_ref[CONV_PAD:2 * CONV_PAD, :] = x[0:CONV_PAD]
        head = convb_ref[:, cs]
        for k in range(SSD_CONV):
            lo = CONV_PAD - CONV_TAIL + k
            head = head + head_ref[lo:lo + CONV_PAD, :] * convw_ref[k:k + 1, cs]
        xc_ref[0:CONV_PAD, cs] = _silu(head)
        tail_ref[:, cs] = x[tl - CONV_PAD:tl]

    a_row = -jnp.exp(alog_ref[...])
    rows = lax.broadcasted_iota(jnp.int32, (q, q), 0)
    cols = lax.broadcasted_iota(jnp.int32, (q, q), 1)
    causal = rows >= cols
    tri = causal.astype(F32)
    low_half = lax.broadcasted_iota(jnp.int32, (q, LANES), 1) < SSD_HEAD_DIM
    head_of_lane = lax.broadcasted_iota(jnp.int32, (q, gw), 1) // SSD_HEAD_DIM
    head_masks = [head_of_lane == r for r in range(SSD_HEADS_PER_GROUP)]

    def per_head_lanes(parts):
        return jnp.concatenate([jnp.where(low_half, parts[0], parts[1]),
                                jnp.where(low_half, parts[2], parts[3])], axis=1)

    def chunk(c, carry):
        r0 = pl.multiple_of(c * q, q)
        rs = pl.ds(r0, q)
        raw = dt_ref[rs, :] + dtb_ref[...]
        dt = jnp.maximum(raw, 0.0) + jnp.log(1.0 + jnp.exp(-jnp.abs(raw)))
        if l_valid is not None:
            row_id = t * tl + r0 + lax.broadcasted_iota(jnp.int32, (q, LANES), 0)
            dt = jnp.where(row_id < l_valid, dt, 0.0)
        da = dt * a_row
        acum = jnp.dot(tri, da, preferred_element_type=F32, precision=lax.Precision.HIGHEST)
        act_ref[...] = acum.T

        for g in range(SSD_GROUPS):
            b0 = SSD_D_INNER + g * SSD_STATE
            c0 = SSD_D_INNER + SSD_GN + g * SSD_STATE
            gs = slice(g * gw, (g + 1) * gw)
            bg = xc_ref[rs, b0:b0 + SSD_STATE].astype(BF16)
            cg = xc_ref[rs, c0:c0 + SSD_STATE].astype(BF16)
            xg = xc_ref[rs, gs]
            cb = lax.dot_general(cg, bg, (((1,), (1,)), ((), ())), preferred_element_type=F32)
            st = st_ref[g]
            y_state = jnp.dot(cg, st.astype(BF16), preferred_element_type=F32)

            heads = range(g * SSD_HEADS_PER_GROUP, (g + 1) * SSD_HEADS_PER_GROUP)
            acols = [jnp.broadcast_to(acum[:, h:h + 1], (q, LANES)) for h in heads]
            acol_g = per_head_lanes(acols)
            dcol_g = per_head_lanes([jnp.broadcast_to(dt[:, h:h + 1], (q, LANES)) for h in heads])
            xdt = xg * dcol_g

            y = jnp.exp(acol_g) * y_state + dskip_ref[:, gs] * xg
            for r, h in enumerate(heads):
                seg = jnp.where(causal, acols[r] - act_ref[h:h + 1, :], MASK_VALUE)
                m = (cb * jnp.exp(seg)).astype(BF16)
                own = jnp.where(head_masks[r], xdt, 0.0).astype(BF16)
                y = y + jnp.dot(m, own, preferred_element_type=F32)

            a_last = acol_g[q - 1:q, :]
            xw = (xdt * jnp.exp(a_last - acol_g)).astype(BF16)
            upd = lax.dot_general(bg, xw, (((0,), (0,)), ((), ())), preferred_element_type=F32)
            st_ref[g] = jnp.exp(a_last) * st + upd

            zg = zx_ref[rs, gs].astype(F32)
            y_ref[rs, gs] = _rms(y * _silu(zg), gn_ref[:, gs]).astype(BF16)
        return carry

    lax.fori_loop(0, tl // q, chunk, 0)

    mix = jnp.dot(y_ref[...], wout_ref[...], preferred_element_type=F32)
    hout_ref[...] = h_ref[...] + _rms(mix, gpost_ref[...])

    @pl.when(t == nt - 1)
    def _():
        n_rows = tl if l_valid is None else (l_valid - 1) % tl + 1
        group = zx_ref[n_rows - CONV_GROUP:n_rows, SSD_D_INNER:].astype(F32)
        cout_ref[...] = group[CONV_GROUP - CONV_TAIL:CONV_GROUP]
        for g in range(SSD_GROUPS):
            sout_ref[g * gw:(g + 1) * gw, :] = st_ref[g].T


def _ssd(zx, dt_raw, h, conv_init, state_init, conv_w, conv_b, dt_bias, a_log, d_skip, gate_norm,
         w_out, g_post, *, l_valid):
    b, l, _ = zx.shape
    tl = _row_tile(l, 256)
    assert tl % SSD_Q == 0 and l - l_valid < tl
    hp = SSD_HEADS * SSD_HEAD_DIM
    kern = functools.partial(_ssd_kernel, tl=tl, l_valid=None if l_valid == l else l_valid)
    return pl.pallas_call(
        kern,
        grid=(b, l // tl),
        in_specs=[
            pl.BlockSpec((None, tl, SSD_ZX_DIM), lambda i, j: (i, j, 0)),
            pl.BlockSpec((None, tl, LANES), lambda i, j: (i, j, 0)),
            pl.BlockSpec((None, tl, D_MODEL), lambda i, j: (i, j, 0)),
            pl.BlockSpec((None, CONV_TAIL, SSD_CONV_DIM), lambda i, j: (i, 0, 0)),
            pl.BlockSpec((None, hp, SSD_STATE), lambda i, j: (i, 0, 0)),
            _resident((SSD_CONV, SSD_CONV_DIM)),
            _resident((1, SSD_CONV_DIM)),
            _resident((1, LANES)),
            _resident((1, LANES)),
            _resident((1, SSD_D_INNER)),
            _resident((1, SSD_D_INNER)),
            _resident((SSD_D_INNER, D_MODEL)),
            _resident((1, D_MODEL)),
        ],
        out_specs=[
            pl.BlockSpec((None, tl, D_MODEL), lambda i, j: (i, j, 0)),
            pl.BlockSpec((None, CONV_TAIL, SSD_CONV_DIM), lambda i, j: (i, 0, 0)),
            pl.BlockSpec((None, hp, SSD_STATE), lambda i, j: (i, 0, 0)),
        ],
        out_shape=[
            jax.ShapeDtypeStruct((b, l, D_MODEL), F32),
            jax.ShapeDtypeStruct((b, CONV_TAIL, SSD_CONV_DIM), F32),
            jax.ShapeDtypeStruct((b, hp, SSD_STATE), F32),
        ],
        scratch_shapes=[
            pltpu.VMEM((CONV_PAD, SSD_CONV_DIM), F32),
            pltpu.VMEM((2 * CONV_PAD, CONV_COLS), F32),
            pltpu.VMEM((tl, SSD_CONV_DIM), F32),
            pltpu.VMEM((SSD_GROUPS, SSD_STATE, SSD_GROUP_WIDTH), F32),
            pltpu.VMEM((tl, SSD_D_INNER), BF16),
            pltpu.VMEM((LANES, SSD_Q), F32),
        ],
        compiler_params=_params("parallel", "arbitrary"),
        name="ssd_scan",
    )(zx, dt_raw, h, conv_init, state_init, conv_w, conv_b, dt_bias, a_log, d_skip, gate_norm,
      w_out, g_post)


FFN_COLS = MXU_WIDTH


def _ffn_kernel(*refs, mixed):
    if mixed:
        a_ref, wmix_ref, gmix_ref, h_ref, gpre_ref, wg_ref, wu_ref, wd_ref, gpost_ref, out_ref = refs
        mix = jnp.dot(a_ref[...], wmix_ref[...], preferred_element_type=F32)
        h = h_ref[...] + _rms(mix, gmix_ref[...])
    else:
        h_ref, gpre_ref, wg_ref, wu_ref, wd_ref, gpost_ref, out_ref = refs
        h = h_ref[...]
    xn = _rms(h, gpre_ref[...]).astype(BF16)
    acc = None
    for f in range(0, FFN_HIDDEN, FFN_COLS):
        gate = jnp.dot(xn, wg_ref[:, f:f + FFN_COLS], preferred_element_type=F32)
        up = jnp.dot(xn, wu_ref[:, f:f + FFN_COLS], preferred_element_type=F32)
        act = (_silu(gate) * up).astype(BF16)
        part = jnp.dot(act, wd_ref[f:f + FFN_COLS, :], preferred_element_type=F32)
        acc = part if acc is None else acc + part
    out_ref[...] = h + _rms(acc, gpost_ref[...])


def _ffn(h, g_pre, w_gate, w_up, w_down, g_post, mix=None):
    t = h.shape[0]
    tm = _row_tile(t, 512)
    rows = lambda width: pl.BlockSpec((tm, width), lambda i: (i, 0))
    mix_specs, mix_args = [], []
    if mix is not None:
        a, w_mix, g_mix = mix
        mix_specs = [rows(a.shape[1]), _resident(w_mix.shape), _resident((1, D_MODEL))]
        mix_args = [a, w_mix, g_mix]
    return pl.pallas_call(
        functools.partial(_ffn_kernel, mixed=mix is not None),
        grid=(t // tm,),
        in_specs=mix_specs + [
            rows(D_MODEL),
            _resident((1, D_MODEL)),
            _resident((D_MODEL, FFN_HIDDEN)),
            _resident((D_MODEL, FFN_HIDDEN)),
            _resident((FFN_HIDDEN, D_MODEL)),
            _resident((1, D_MODEL)),
        ],
        out_specs=rows(D_MODEL),
        out_shape=jax.ShapeDtypeStruct((t, D_MODEL), F32),
        compiler_params=_params("parallel"),
        name="mix_ffn" if mix is not None else "ffn",
    )(*mix_args, h, g_pre, w_gate, w_up, w_down, g_post)


def _rope_slot(x, cos, sin_hi, sin_lo):
    return (x * cos + pltpu.roll(x, MLA_ROPE_HALF, axis=1) * sin_hi
            + pltpu.roll(x, LANES - MLA_ROPE_HALF, axis=1) * sin_lo)


def _mla_pre_kernel(h_ref, gpre_ref, wqa_ref, qn_ref, wqb_ref, wlat_ref, kvn_ref, wkr_ref,
                    cos_ref, shi_ref, slo_ref, q_ref, lat_ref, kr_ref):
    xn = _rms(h_ref[...], gpre_ref[...]).astype(BF16)
    cos, s_hi, s_lo = cos_ref[...], shi_ref[...], slo_ref[...]

    qa = jnp.dot(xn, wqa_ref[...], preferred_element_type=F32)
    qa = _rms(qa, qn_ref[...]).astype(BF16)
    for c0 in range(0, MLA_Q_WIDTH, MXU_WIDTH):
        qb = jnp.dot(qa, wqb_ref[:, c0:c0 + MXU_WIDTH], preferred_element_type=F32)
        for s0 in range(0, MXU_WIDTH, HEAD_SLOT):
            qh = _rope_slot(qb[:, s0:s0 + HEAD_SLOT], cos, s_hi, s_lo) * MLA_SCALE_LOG2
            q_ref[:, c0 + s0:c0 + s0 + HEAD_SLOT] = qh.astype(BF16)

    lat = jnp.dot(xn, wlat_ref[...], preferred_element_type=F32)
    lat_ref[...] = _rms(lat, kvn_ref[...])
    kr = _rope_slot(jnp.dot(xn, wkr_ref[...], preferred_element_type=F32), cos, s_hi, s_lo)
    kr_ref[...] = pltpu.roll(kr, LANES - ROPE_LO, axis=1)[:, :MLA_ROPE]


def _mla_pre(h, g_pre, wq_a, q_norm, wq_b, w_lat, kv_norm, w_kr, cos, s_hi, s_lo):
    b, l, _ = h.shape
    tm = _row_tile(l, 512)
    row = lambda i, j: (i, j, 0)
    tab = pl.BlockSpec((tm, LANES), lambda i, j: (j, 0))
    return pl.pallas_call(
        _mla_pre_kernel,
        grid=(b, l // tm),
        in_specs=[
            pl.BlockSpec((None, tm, D_MODEL), row),
            _resident((1, D_MODEL)),
            _resident((D_MODEL, MLA_Q_LORA)),
            _resident((1, MLA_Q_LORA)),
            _resident((MLA_Q_LORA, MLA_Q_WIDTH)),
            _resident((D_MODEL, MLA_KV_LORA)),
            _resident((1, MLA_KV_LORA)),
            _resident((D_MODEL, LANES)),
            tab, tab, tab,
        ],
        out_specs=[
            pl.BlockSpec((None, tm, MLA_Q_WIDTH), row),
            pl.BlockSpec((None, tm, MLA_KV_LORA), row),
            pl.BlockSpec((None, tm, MLA_ROPE), row),
        ],
        out_shape=[
            jax.ShapeDtypeStruct((b, l, MLA_Q_WIDTH), BF16),
            jax.ShapeDtypeStruct((b, l, MLA_KV_LORA), F32),
            jax.ShapeDtypeStruct((b, l, MLA_ROPE), F32),
        ],
        compiler_params=_params("parallel", "parallel"),
        name="mla_pre",
    )(h, g_pre, wq_a, q_norm, wq_b, w_lat, kv_norm, w_kr, cos, s_hi, s_lo)


def _kv_expand_kernel(lat_ref, kr_ref, wk_ref, wv_ref, place_ref, k_ref, v_ref):
    lat = lat_ref[...].astype(BF16)
    kr_slot = jnp.dot(kr_ref[...].astype(BF16), place_ref[...], preferred_element_type=F32)
    ones_lane = (lax.broadcasted_iota(jnp.int32, (1, HEAD_SLOT), 1) == MLA_V).astype(F32)
    kr_pair = jnp.concatenate([kr_slot, kr_slot], axis=1)
    ones_pair = jnp.concatenate([ones_lane, ones_lane], axis=1)
    for c0 in range(0, MLA_Q_WIDTH, MXU_WIDTH):
        sl = slice(c0, c0 + MXU_WIDTH)
        kb = jnp.dot(lat, wk_ref[:, sl], preferred_element_type=F32)
        k_ref[:, sl] = (kb + kr_pair).astype(BF16)
        vb = jnp.dot(lat, wv_ref[:, sl], preferred_element_type=F32)
        v_ref[:, sl] = (vb + ones_pair).astype(BF16)


def _kv_expand(lat, kr, w_k, w_v, place):
    t = lat.shape[0]
    tm = _row_tile(t, 512)
    return pl.pallas_call(
        _kv_expand_kernel,
        grid=(t // tm,),
        in_specs=[
            pl.BlockSpec((tm, MLA_KV_LORA), lambda i: (i, 0)),
            pl.BlockSpec((tm, MLA_ROPE), lambda i: (i, 0)),
            _resident((MLA_KV_LORA, MLA_Q_WIDTH)),
            _resident((MLA_KV_LORA, MLA_Q_WIDTH)),
            _resident((MLA_ROPE, LANES)),
        ],
        out_specs=[
            pl.BlockSpec((tm, MLA_Q_WIDTH), lambda i: (i, 0)),
            pl.BlockSpec((tm, MLA_Q_WIDTH), lambda i: (i, 0)),
        ],
        out_shape=[
            jax.ShapeDtypeStruct((t, MLA_Q_WIDTH), BF16),
            jax.ShapeDtypeStruct((t, MLA_Q_WIDTH), BF16),
        ],
        compiler_params=_params("parallel"),
        name="mla_kv_expand",
    )(lat, kr, w_k, w_v, place)


ATTN_SUB_K = 256
ATTN_SMALL_Q = 64


def _kv_tiles_needed(qi, *, tq, tk, q_off, lk):
    last_chunk = (q_off + qi * tq + tq - 1) // CHUNK
    k_end = min((last_chunk + 1) * CHUNK, lk)
    return (k_end + tk - 1) // tk


def _attn_kernel(qi_ref, ki_ref, last_ref, q_ref, k_ref, v_ref, o_ref, m_ref, acc_ref,
                 *, tq, tk, sub_k, q_off, lk):
    step_id = pl.program_id(1)
    qi = qi_ref[step_id]
    ki = ki_ref[step_id]

    @pl.when(ki == 0)
    def _():
        m_ref[...] = jnp.full(m_ref.shape, MASK_VALUE, F32)
        acc_ref[...] = jnp.zeros(acc_ref.shape, F32)

    first_chunk = (q_off + qi * tq) // CHUNK
    tile_last_chunk = (ki * tk + tk - 1) // CHUNK
    partial = jnp.logical_or(tile_last_chunk > first_chunk, ki * tk + tk > lk)

    def step(masked):
        if masked:
            q_chunk = (q_off + qi * tq + lax.broadcasted_iota(jnp.int32, (tq, tk), 0)) // CHUNK
            k_pos = ki * tk + lax.broadcasted_iota(jnp.int32, (tq, tk), 1)
            visible = jnp.logical_and(k_pos // CHUNK <= q_chunk, k_pos < lk)
        for hd in range(MLA_HEADS):
            sl = slice(hd * HEAD_SLOT, (hd + 1) * HEAD_SLOT)
            for k0 in range(0, tk, sub_k):
                ks = slice(k0, k0 + sub_k)
                s = lax.dot_general(q_ref[:, sl], k_ref[ks, sl], (((1,), (1,)), ((), ())),
                                    preferred_element_type=F32)
                if masked:
                    s = jnp.where(visible[:, ks], s, MASK_VALUE)
                m_prev = m_ref[hd]
                m_new = jnp.maximum(m_prev, jnp.max(s, axis=-1, keepdims=True))
                alpha = jnp.exp2(m_prev - m_new)
                p = jnp.exp2((s - jnp.tile(m_new, (1, sub_k // LANES))).astype(BF16))
                m_ref[hd] = m_new
                pv = jnp.dot(p, v_ref[ks, sl], preferred_element_type=F32)
                acc_ref[hd] = alpha * acc_ref[hd] + pv

    @pl.when(partial)
    def _():
        step(True)

    @pl.when(jnp.logical_not(partial))
    def _():
        step(False)

    @pl.when(last_ref[step_id] == 1)
    def _():
        low_half = lax.broadcasted_iota(jnp.int32, (tq, LANES), 1) < MLA_V
        for pair in range(MLA_HEADS // 2):
            even = acc_ref[2 * pair]
            odd = acc_ref[2 * pair + 1]
            even = even / even[:, MLA_V:MLA_V + 1]
            odd = odd / odd[:, MLA_V:MLA_V + 1]
            packed = jnp.where(low_half, even, pltpu.roll(odd, MLA_V, axis=1))
            o_ref[:, pair * LANES:(pair + 1) * LANES] = packed.astype(BF16)


def _attention(q, k, v, *, q_off, lk):
    b, lq, _ = q.shape
    lk_pad = k.shape[1]
    tq = _row_tile(lq, 512)
    tk = lk_pad if lk_pad <= 1536 else 512
    assert lk_pad % tk == 0 and tk % LANES == 0
    static = dict(tq=tq, tk=tk, q_off=q_off, lk=lk)

    qi_tab, ki_tab, last_tab = [], [], []
    for qi in range(lq // tq):
        needed = _kv_tiles_needed(qi, **static)
        qi_tab += [qi] * needed
        ki_tab += list(range(needed))
        last_tab += [0] * (needed - 1) + [1]
    tables = [jnp.asarray(tab, jnp.int32) for tab in (qi_tab, ki_tab, last_tab)]

    q_index = lambda i, s, qi_ref, ki_ref, last_ref: (i, qi_ref[s], 0)
    k_index = lambda i, s, qi_ref, ki_ref, last_ref: (i, ki_ref[s], 0)
    return pl.pallas_call(
        functools.partial(_attn_kernel, sub_k=ATTN_SUB_K if tq > ATTN_SMALL_Q else tk, **static),
        grid_spec=pltpu.PrefetchScalarGridSpec(
            num_scalar_prefetch=3,
            grid=(b, len(qi_tab)),
            in_specs=[
                pl.BlockSpec((None, tq, MLA_Q_WIDTH), q_index),
                pl.BlockSpec((None, tk, MLA_Q_WIDTH), k_index),
                pl.BlockSpec((None, tk, MLA_Q_WIDTH), k_index),
            ],
            out_specs=pl.BlockSpec((None, tq, MLA_V_WIDTH), q_index),
            scratch_shapes=[
                pltpu.VMEM((MLA_HEADS, tq, LANES), F32),
                pltpu.VMEM((MLA_HEADS, tq, LANES), F32),
            ],
        ),
        out_shape=jax.ShapeDtypeStruct((b, lq, MLA_V_WIDTH), BF16),
        compiler_params=_params("parallel", "arbitrary"),
        name="mla_attention",
    )(*tables, q, k, v)


def _prep_params(ln_mix_pre, ln_mix_post, ln_ffn_pre, ln_ffn_post,
                 ssd_w_in, ssd_conv_w, ssd_conv_b, ssd_dt_bias, ssd_a_log, ssd_d, ssd_gate_norm, ssd_w_out,
                 mla_wq_a, mla_q_norm, mla_wq_b, mla_wkv_a, mla_kv_norm, mla_w_uk, mla_w_uv, mla_w_o,
                 ffn_w_gate, ffn_w_up, ffn_w_down):
    row = lambda v: v.reshape(1, -1).astype(F32)
    pad_lanes = lambda v: jnp.pad(v, ((0, 0), (0, LANES - v.shape[1])))
    head_slots = lambda w, d: jnp.pad(
        w.reshape(w.shape[0], MLA_HEADS, d), ((0, 0), (0, 0), (0, HEAD_SLOT - d))).reshape(w.shape[0], -1)
    w_in = ssd_w_in[0]
    w_kr = jnp.zeros((D_MODEL, LANES), F32).at[:, ROPE_LO:ROPE_LO + MLA_ROPE].set(mla_wkv_a[0][:, MLA_KV_LORA:])
    place = jnp.zeros((MLA_ROPE, LANES), F32).at[jnp.arange(MLA_ROPE), ROPE_LO + jnp.arange(MLA_ROPE)].set(1.0)
    return dict(
        ln_mix_pre=[row(ln_mix_pre[i]) for i in range(2)],
        ln_mix_post=[row(ln_mix_post[i]) for i in range(2)],
        ln_ffn_pre=[row(ln_ffn_pre[i]) for i in range(2)],
        ln_ffn_post=[row(ln_ffn_post[i]) for i in range(2)],
        w_zx=w_in[:, :SSD_ZX_DIM].astype(BF16),
        w_dt=pad_lanes(w_in[:, SSD_ZX_DIM:]).astype(BF16),
        conv_w=ssd_conv_w[0].astype(F32),
        conv_b=row(ssd_conv_b[0]),
        dt_bias=pad_lanes(row(ssd_dt_bias[0])),
        a_log=pad_lanes(row(ssd_a_log[0])),
        d_skip=row(jnp.repeat(ssd_d[0], SSD_HEAD_DIM)),
        gate_norm=row(ssd_gate_norm[0]),
        w_out=ssd_w_out[0].astype(BF16),
        wq_a=mla_wq_a[0].astype(BF16),
        q_norm=row(mla_q_norm[0]),
        wq_b=head_slots(mla_wq_b[0], MLA_QK).astype(BF16),
        w_lat=mla_wkv_a[0][:, :MLA_KV_LORA].astype(BF16),
        kv_norm=row(mla_kv_norm[0]),
        w_kr=w_kr.astype(BF16),
        w_k=head_slots(mla_w_uk[0].reshape(MLA_KV_LORA, -1), MLA_NOPE).astype(BF16),
        w_v=head_slots(mla_w_uv[0].reshape(MLA_KV_LORA, -1), MLA_V).astype(BF16),
        place=place.astype(BF16),
        w_o=mla_w_o[0].astype(BF16),
        ffn_w_gate=[ffn_w_gate[i].astype(BF16) for i in range(2)],
        ffn_w_up=[ffn_w_up[i].astype(BF16) for i in range(2)],
        ffn_w_down=[ffn_w_down[i].astype(BF16) for i in range(2)],
    )


def _rope_tables(pos):
    inv = ROPE_THETA ** (-jnp.arange(MLA_ROPE_HALF, dtype=F32) / MLA_ROPE_HALF)
    ang = pos.astype(F32)[:, None] * inv[None, :]
    cos, sin = jnp.cos(ang), jnp.sin(ang)
    l = pos.shape[0]
    ones = jnp.ones((l, ROPE_LO), F32)
    tail = jnp.ones((l, LANES - ROPE_LO - MLA_ROPE), F32)
    zeros_lo = jnp.zeros((l, ROPE_LO), F32)
    zeros_half = jnp.zeros((l, MLA_ROPE_HALF), F32)
    zeros_tail = jnp.zeros((l, LANES - ROPE_LO - MLA_ROPE), F32)
    cos_t = jnp.concatenate([ones, cos, cos, tail], axis=1)
    s_hi = jnp.concatenate([zeros_lo, zeros_half, sin, zeros_tail], axis=1)
    s_lo = jnp.concatenate([zeros_lo, -sin, zeros_half, zeros_tail], axis=1)
    return cos_t, s_hi, s_lo


def _round_up(n, m):
    return -(-n // m) * m


def _trunk(x, conv_init, state_init, past_lat, past_kr, p):
    b, l, _ = x.shape
    t = b * l
    past = past_lat.shape[1]
    flat = lambda a: a.reshape(t, a.shape[-1])

    zx, dt_raw = _inproj(flat(x), p['ln_mix_pre'][0], p['w_zx'], p['w_dt'])
    l_pad = _round_up(l, SSD_Q)
    seq = lambda a: jnp.pad(a.reshape(b, l, a.shape[-1]), ((0, 0), (0, l_pad - l), (0, 0)))
    h1, new_conv, new_state = _ssd(
        seq(zx), seq(dt_raw), seq(flat(x)), conv_init, state_init.reshape(b, -1, SSD_STATE),
        p['conv_w'], p['conv_b'], p['dt_bias'], p['a_log'], p['d_skip'], p['gate_norm'],
        p['w_out'], p['ln_mix_post'][0], l_valid=l)
    h1 = flat(h1[:, :l])
    h2 = _ffn(h1, p['ln_ffn_pre'][0], p['ffn_w_gate'][0], p['ffn_w_up'][0], p['ffn_w_down'][0],
              p['ln_ffn_post'][0])

    cos, s_hi, s_lo = _rope_tables(past + jnp.arange(l, dtype=jnp.int32))
    q, lat, kr = _mla_pre(h2.reshape(b, l, D_MODEL), p['ln_mix_pre'][1], p['wq_a'], p['q_norm'], p['wq_b'],
                          p['w_lat'], p['kv_norm'], p['w_kr'], cos, s_hi, s_lo)
    lk = past + l
    lk_pad = _round_up(lk, ATTN_SUB_K if l > ATTN_SMALL_Q else LANES)
    keys = lambda new, old: jnp.pad(jnp.concatenate([old.astype(F32), new], axis=1),
                                    ((0, 0), (0, lk_pad - lk), (0, 0)))
    k_all, v_all = _kv_expand(keys(lat, past_lat).reshape(b * lk_pad, -1),
                              keys(kr, past_kr).reshape(b * lk_pad, -1), p['w_k'], p['w_v'], p['place'])
    o = _attention(q, k_all.reshape(b, lk_pad, -1), v_all.reshape(b, lk_pad, -1), q_off=past, lk=lk)
    h4 = _ffn(h2, p['ln_ffn_pre'][1], p['ffn_w_gate'][1], p['ffn_w_up'][1], p['ffn_w_down'][1],
              p['ln_ffn_post'][1], mix=(flat(o), p['w_o'], p['ln_mix_post'][1]))

    return (h4.reshape(b, l, D_MODEL), new_conv[None],
            new_state.reshape(b, SSD_HEADS, SSD_HEAD_DIM, SSD_STATE)[None], lat[None], kr[None])


def kernel(x_prompt, x_sample, state_ssd_conv, state_ssd_ssm, cache_mla_latent, cache_mla_krope,
           ln_mix_pre, ln_mix_post, ln_ffn_pre, ln_ffn_post,
           ssd_w_in, ssd_conv_w, ssd_conv_b, ssd_dt_bias, ssd_a_log, ssd_d, ssd_gate_norm, ssd_w_out,
           mla_wq_a, mla_q_norm, mla_wq_b, mla_wkv_a, mla_kv_norm, mla_w_uk, mla_w_uv, mla_w_o,
           ffn_w_gate, ffn_w_up, ffn_w_down):
    p = _prep_params(ln_mix_pre, ln_mix_post, ln_ffn_pre, ln_ffn_post,
                     ssd_w_in, ssd_conv_w, ssd_conv_b, ssd_dt_bias, ssd_a_log, ssd_d, ssd_gate_norm,
                     ssd_w_out, mla_wq_a, mla_q_norm, mla_wq_b, mla_wkv_a, mla_kv_norm, mla_w_uk,
                     mla_w_uv, mla_w_o, ffn_w_gate, ffn_w_up, ffn_w_down)
    b = x_prompt.shape[0]
    y_p, p_conv, p_ssm, p_lat, p_kr = _trunk(
        x_prompt,
        jnp.zeros((b, CONV_TAIL, SSD_CONV_DIM), F32),
        jnp.zeros((b, SSD_HEADS, SSD_HEAD_DIM, SSD_STATE), F32),
        jnp.zeros((b, 0, MLA_KV_LORA), F32),
        jnp.zeros((b, 0, MLA_ROPE), F32), p)
    y_s, s_conv, s_ssm, s_lat, s_kr = _trunk(
        x_sample, state_ssd_conv[0], state_ssd_ssm[0], cache_mla_latent[0], cache_mla_krope[0], p)
    return (y_p, y_s, p_conv, p_ssm, p_lat, p_kr, s_conv, s_ssm, s_lat, s_kr)
```

```python
import functools
import math

import jax
import jax.numpy as jnp
from jax import lax
from jax.experimental import pallas as pl
from jax.experimental.pallas import tpu as pltpu

F32 = jnp.float32
BF16 = jnp.bfloat16

D_MODEL = 1024
NORM_EPS = 1e-6
CHUNK = 64

SSD_D_INNER = 2048
SSD_HEAD_DIM = 64
SSD_HEADS = 32
SSD_GROUPS = 8
SSD_HEADS_PER_GROUP = 4
SSD_STATE = 128
SSD_CONV = 4
SSD_GN = SSD_GROUPS * SSD_STATE
SSD_CONV_DIM = SSD_D_INNER + 2 * SSD_GN
SSD_ZX_DIM = SSD_D_INNER + SSD_CONV_DIM
SSD_GROUP_WIDTH = SSD_HEADS_PER_GROUP * SSD_HEAD_DIM

MLA_HEADS = 16
MLA_Q_LORA = 512
MLA_KV_LORA = 256
MLA_NOPE = 64
MLA_ROPE = 32
MLA_ROPE_HALF = MLA_ROPE // 2
MLA_QK = MLA_NOPE + MLA_ROPE
MLA_V = 64
MLA_SCALE = 1.0 / math.sqrt(MLA_QK)
MLA_SCALE_LOG2 = MLA_SCALE * math.log2(math.e)
ROPE_THETA = 10000.0

FFN_HIDDEN = 2816

LANES = 128
MXU_WIDTH = 256
VMEM_LIMIT_BYTES = 56 * 1024 * 1024

HEAD_SLOT = LANES
ROPE_LO = MLA_NOPE
ROPE_HI = MLA_NOPE + MLA_ROPE_HALF
MLA_Q_WIDTH = MLA_HEADS * HEAD_SLOT
MLA_V_WIDTH = MLA_HEADS * MLA_V
MASK_VALUE = -1e30


def _rms(x, g):
    return x * lax.rsqrt(jnp.mean(x * x, axis=-1, keepdims=True) + NORM_EPS) * g


def _silu_of_twice(half):
    return half + half * jnp.tanh(half)


def _silu(x):
    return _silu_of_twice(0.5 * x)


def _resident(shape):
    zeros = (0,) * len(shape)
    return pl.BlockSpec(shape, lambda *_: zeros, pipeline_mode=pl.Buffered(1))


def _params(*semantics):
    return pltpu.CompilerParams(dimension_semantics=semantics, vmem_limit_bytes=VMEM_LIMIT_BYTES)


def _row_tile(n, target):
    if n <= target:
        return n
    for t in range(target, 0, -16):
        if n % t == 0:
            return t
    raise ValueError((n, target))


INPROJ_COLS = 512


def _inproj_kernel(x_ref, g_ref, w_ref, wdt_ref, zx_ref, dt_ref):
    xn = _rms(x_ref[...], g_ref[...]).astype(BF16)
    for n in range(0, SSD_ZX_DIM, INPROJ_COLS):
        zx_ref[:, n:n + INPROJ_COLS] = jnp.dot(
            xn, w_ref[:, n:n + INPROJ_COLS], preferred_element_type=F32).astype(BF16)
    dt_ref[...] = jnp.dot(xn, wdt_ref[...], preferred_element_type=F32)


def _inproj(x, g, w_zx, w_dt):
    t = x.shape[0]
    tm = _row_tile(t, 512)
    return pl.pallas_call(
        _inproj_kernel,
        grid=(t // tm,),
        in_specs=[
            pl.BlockSpec((tm, D_MODEL), lambda i: (i, 0)),
            _resident((1, D_MODEL)),
            _resident((D_MODEL, SSD_ZX_DIM)),
            _resident((D_MODEL, LANES)),
        ],
        out_specs=[
            pl.BlockSpec((tm, SSD_ZX_DIM), lambda i: (i, 0)),
            pl.BlockSpec((tm, LANES), lambda i: (i, 0)),
        ],
        out_shape=[
            jax.ShapeDtypeStruct((t, SSD_ZX_DIM), BF16),
            jax.ShapeDtypeStruct((t, LANES), F32),
        ],
        compiler_params=_params("parallel"),
        name="ssd_inproj",
    )(x, g, w_zx, w_dt)


CONV_TAIL = SSD_CONV - 1
CONV_PAD = 8
CONV_GROUP = 16
CONV_COLS = 512
SSD_Q = LANES


def _ssd_kernel(zx_ref, dt_ref, h_ref, cinit_ref, sinit_ref, convw_ref, convb_ref, dtb_ref,
                alog_ref, dskip_ref, gn_ref, wout_ref, gpost_ref,
                hout_ref, cout_ref, sout_ref,
                tail_ref, head_ref, xc_ref, st_ref, y_ref, act_ref,
                *, tl, l_valid):
    t = pl.program_id(1)
    nt = pl.num_programs(1)
    q = SSD_Q
    gw = SSD_GROUP_WIDTH

    @pl.when(t == 0)
    def _():
        tail_ref[...] = jnp.zeros((CONV_PAD, SSD_CONV_DIM), F32)
        tail_ref[CONV_PAD - CONV_TAIL:CONV_PAD, :] = cinit_ref[...]
        for g in range(SSD_GROUPS):
            st_ref[g] = sinit_ref[g * gw:(g + 1) * gw, :].T

    srow = lax.broadcasted_iota(jnp.int32, (tl, tl), 0)
    scol = lax.broadcasted_iota(jnp.int32, (tl, tl), 1)
    shifts = [(srow - scol == j).astype(BF16) for j in range(1, SSD_CONV)]
    for c0 in range(0, SSD_CONV_DIM, CONV_COLS):
        cs = slice(c0, c0 + CONV_COLS)
        xb = zx_ref[:, SSD_D_INNER + c0:SSD_D_INNER + c0 + CONV_COLS]
        x = xb.astype(F32)
        w_half = 0.5 * convw_ref[:, cs]
        b_half = 0.5 * convb_ref[:, cs]
        acc = b_half + x * w_half[SSD_CONV - 1:SSD_CONV]
        for j in range(1, SSD_CONV):
            shifted = jnp.dot(shifts[j - 1], xb, preferred_element_type=F32)
            acc = acc + shifted * w_half[SSD_CONV - 1 - j:SSD_CONV - j]
        xc_ref[:, cs] = _silu_of_twice(acc)

        head_ref[0:CONV_PAD, :] = tail_ref[:, cs]
        head_ref[CONV_PAD:2 * CONV_PAD, :] = x[0:CONV_PAD]
        head = b_half
        for k in range(SSD_CONV):
            lo = CONV_PAD - CONV_TAIL + k
            head = head + head_ref[lo:lo + CONV_PAD, :] * w_half[k:k + 1]
        xc_ref[0:CONV_PAD, cs] = _silu_of_twice(head)
        tail_ref[:, cs] = x[tl - CONV_PAD:tl]

    a_row = -jnp.exp(alog_ref[...])
    rows = lax.broadcasted_iota(jnp.int32, (q, q), 0)
    cols = lax.broadcasted_iota(jnp.int32, (q, q), 1)
    causal = rows >= cols
    tri = causal.astype(F32)
    low_half = lax.broadcasted_iota(jnp.int32, (q, LANES), 1) < SSD_HEAD_DIM
    head_of_lane = lax.broadcasted_iota(jnp.int32, (q, gw), 1) // SSD_HEAD_DIM
    head_masks = [head_of_lane == r for r in range(SSD_HEADS_PER_GROUP)]

    def per_head_lanes(parts):
        return jnp.concatenate([jnp.where(low_half, parts[0], parts[1]),
                                jnp.where(low_half, parts[2], parts[3])], axis=1)

    def chunk(c):
        r0 = c * q
        rs = slice(r0, r0 + q)
        raw = dt_ref[rs, :] + dtb_ref[...]
        dt = jnp.maximum(raw, 0.0) + jnp.log(1.0 + jnp.exp(-jnp.abs(raw)))
        if l_valid is not None:
            row_id = t * tl + r0 + lax.broadcasted_iota(jnp.int32, (q, LANES), 0)
            dt = jnp.where(row_id < l_valid, dt, 0.0)
        da = dt * a_row
        acum = jnp.dot(tri, da, preferred_element_type=F32, precision=lax.Precision.HIGHEST)
        act_ref[c] = acum.T

        for g in range(SSD_GROUPS):
            b0 = SSD_D_INNER + g * SSD_STATE
            c0 = SSD_D_INNER + SSD_GN + g * SSD_STATE
            gs = slice(g * gw, (g + 1) * gw)
            bg = xc_ref[rs, b0:b0 + SSD_STATE].astype(BF16)
            cg = xc_ref[rs, c0:c0 + SSD_STATE].astype(BF16)
            xg = xc_ref[rs, gs]
            cb = lax.dot_general(cg, bg, (((1,), (1,)), ((), ())), preferred_element_type=F32)
            st = st_ref[g]
            y_state = jnp.dot(cg, st.astype(BF16), preferred_element_type=F32)

            heads = range(g * SSD_HEADS_PER_GROUP, (g + 1) * SSD_HEADS_PER_GROUP)
            acols = [jnp.broadcast_to(acum[:, h:h + 1], (q, LANES)) for h in heads]
            acol_g = per_head_lanes(acols)
            dcol_g = per_head_lanes([jnp.broadcast_to(dt[:, h:h + 1], (q, LANES)) for h in heads])
            xdt = xg * dcol_g

            y = jnp.exp(acol_g) * y_state + dskip_ref[:, gs] * xg
            for r, h in enumerate(heads):
                seg = jnp.where(causal, acols[r] - act_ref[c, h:h + 1, :], MASK_VALUE)
                m = (cb * jnp.exp(seg)).astype(BF16)
                own = jnp.where(head_masks[r], xdt, 0.0).astype(BF16)
                y = y + jnp.dot(m, own, preferred_element_type=F32)

            a_last = acol_g[q - 1:q, :]
            xw = (xdt * jnp.exp(a_last - acol_g)).astype(BF16)
            upd = lax.dot_general(bg, xw, (((0,), (0,)), ((), ())), preferred_element_type=F32)
            st_ref[g] = jnp.exp(a_last) * st + upd

            zg = zx_ref[rs, gs].astype(F32)
            y_ref[rs, gs] = _rms(y * _silu(zg), gn_ref[:, gs]).astype(BF16)

    for c in range(tl // q):
        chunk(c)

    mix = jnp.dot(y_ref[...], wout_ref[...], preferred_element_type=F32)
    hout_ref[...] = h_ref[...] + _rms(mix, gpost_ref[...])

    @pl.when(t == nt - 1)
    def _():
        n_rows = tl if l_valid is None else (l_valid - 1) % tl + 1
        group = zx_ref[n_rows - CONV_GROUP:n_rows, SSD_D_INNER:].astype(F32)
        cout_ref[...] = group[CONV_GROUP - CONV_TAIL:CONV_GROUP]
        for g in range(SSD_GROUPS):
            sout_ref[g * gw:(g + 1) * gw, :] = st_ref[g].T


def _ssd(zx, dt_raw, h, conv_init, state_init, conv_w, conv_b, dt_bias, a_log, d_skip, gate_norm,
         w_out, g_post, *, l_valid):
    b, l, _ = zx.shape
    tl = _row_tile(l, 256)
    assert tl % SSD_Q == 0 and l - l_valid < tl
    hp = SSD_HEADS * SSD_HEAD_DIM
    kern = functools.partial(_ssd_kernel, tl=tl, l_valid=None if l_valid == l else l_valid)
    return pl.pallas_call(
        kern,
        grid=(b, l // tl),
        in_specs=[
            pl.BlockSpec((None, tl, SSD_ZX_DIM), lambda i, j: (i, j, 0)),
            pl.BlockSpec((None, tl, LANES), lambda i, j: (i, j, 0)),
            pl.BlockSpec((None, tl, D_MODEL), lambda i, j: (i, j, 0)),
            pl.BlockSpec((None, CONV_TAIL, SSD_CONV_DIM), lambda i, j: (i, 0, 0)),
            pl.BlockSpec((None, hp, SSD_STATE), lambda i, j: (i, 0, 0)),
            _resident((SSD_CONV, SSD_CONV_DIM)),
            _resident((1, SSD_CONV_DIM)),
            _resident((1, LANES)),
            _resident((1, LANES)),
            _resident((1, SSD_D_INNER)),
            _resident((1, SSD_D_INNER)),
            _resident((SSD_D_INNER, D_MODEL)),
            _resident((1, D_MODEL)),
        ],
        out_specs=[
            pl.BlockSpec((None, tl, D_MODEL), lambda i, j: (i, j, 0)),
            pl.BlockSpec((None, CONV_TAIL, SSD_CONV_DIM), lambda i, j: (i, 0, 0)),
            pl.BlockSpec((None, hp, SSD_STATE), lambda i, j: (i, 0, 0)),
        ],
        out_shape=[
            jax.ShapeDtypeStruct((b, l, D_MODEL), F32),
            jax.ShapeDtypeStruct((b, CONV_TAIL, SSD_CONV_DIM), F32),
            jax.ShapeDtypeStruct((b, hp, SSD_STATE), F32),
        ],
        scratch_shapes=[
            pltpu.VMEM((CONV_PAD, SSD_CONV_DIM), F32),
            pltpu.VMEM((2 * CONV_PAD, CONV_COLS), F32),
            pltpu.VMEM((tl, SSD_CONV_DIM), F32),
            pltpu.VMEM((SSD_GROUPS, SSD_STATE, SSD_GROUP_WIDTH), F32),
            pltpu.VMEM((tl, SSD_D_INNER), BF16),
            pltpu.VMEM((tl // SSD_Q, LANES, SSD_Q), F32),
        ],
        compiler_params=_params("parallel", "arbitrary"),
        name="ssd_scan",
    )(zx, dt_raw, h, conv_init, state_init, conv_w, conv_b, dt_bias, a_log, d_skip, gate_norm,
      w_out, g_post)


FFN_COLS = MXU_WIDTH


def _ffn_kernel(*refs, mixed):
    if mixed:
        a_ref, wmix_ref, gmix_ref, h_ref, gpre_ref, wg_ref, wu_ref, wd_ref, gpost_ref, out_ref = refs
        mix = jnp.dot(a_ref[...], wmix_ref[...], preferred_element_type=F32)
        h = h_ref[...] + _rms(mix, gmix_ref[...])
    else:
        h_ref, gpre_ref, wg_ref, wu_ref, wd_ref, gpost_ref, out_ref = refs
        h = h_ref[...]
    xn = _rms(h, gpre_ref[...]).astype(BF16)
    acc = None
    for f in range(0, FFN_HIDDEN, FFN_COLS):
        gate = jnp.dot(xn, wg_ref[:, f:f + FFN_COLS], preferred_element_type=F32)
        up = jnp.dot(xn, wu_ref[:, f:f + FFN_COLS], preferred_element_type=F32)
        act = (_silu(gate) * up).astype(BF16)
        part = jnp.dot(act, wd_ref[f:f + FFN_COLS, :], preferred_element_type=F32)
        acc = part if acc is None else acc + part
    out_ref[...] = h + _rms(acc, gpost_ref[...])


def _ffn(h, g_pre, w_gate, w_up, w_down, g_post, mix=None):
    t = h.shape[0]
    tm = _row_tile(t, 512)
    rows = lambda width: pl.BlockSpec((tm, width), lambda i: (i, 0))
    mix_specs, mix_args = [], []
    if mix is not None:
        a, w_mix, g_mix = mix
        mix_specs = [rows(a.shape[1]), _resident(w_mix.shape), _resident((1, D_MODEL))]
        mix_args = [a, w_mix, g_mix]
    return pl.pallas_call(
        functools.partial(_ffn_kernel, mixed=mix is not None),
        grid=(t // tm,),
        in_specs=mix_specs + [
            rows(D_MODEL),
            _resident((1, D_MODEL)),
            _resident((D_MODEL, FFN_HIDDEN)),
            _resident((D_MODEL, FFN_HIDDEN)),
            _resident((FFN_HIDDEN, D_MODEL)),
            _resident((1, D_MODEL)),
        ],
        out_specs=rows(D_MODEL),
        out_shape=jax.ShapeDtypeStruct((t, D_MODEL), F32),
        compiler_params=_params("parallel"),
        name="mix_ffn" if mix is not None else "ffn",
    )(*mix_args, h, g_pre, w_gate, w_up, w_down, g_post)


def _rope_slot(x, cos, sin_hi, sin_lo):
    return (x * cos + pltpu.roll(x, MLA_ROPE_HALF, axis=1) * sin_hi
            + pltpu.roll(x, LANES - MLA_ROPE_HALF, axis=1) * sin_lo)


def _mla_pre_kernel(h_ref, gpre_ref, wqa_ref, qn_ref, wqb_ref, wlat_ref, kvn_ref, wkr_ref,
                    cos_ref, shi_ref, slo_ref, q_ref, lat_ref, kr_ref):
    xn = _rms(h_ref[...], gpre_ref[...]).astype(BF16)
    cos, s_hi, s_lo = cos_ref[...], shi_ref[...], slo_ref[...]

    qa = jnp.dot(xn, wqa_ref[...], preferred_element_type=F32)
    qa = _rms(qa, qn_ref[...]).astype(BF16)
    for c0 in range(0, MLA_Q_WIDTH, MXU_WIDTH):
        qb = jnp.dot(qa, wqb_ref[:, c0:c0 + MXU_WIDTH], preferred_element_type=F32)
        for s0 in range(0, MXU_WIDTH, HEAD_SLOT):
            qh = _rope_slot(qb[:, s0:s0 + HEAD_SLOT], cos, s_hi, s_lo) * MLA_SCALE_LOG2
            q_ref[:, c0 + s0:c0 + s0 + HEAD_SLOT] = qh.astype(BF16)

    lat = jnp.dot(xn, wlat_ref[...], preferred_element_type=F32)
    lat_ref[...] = _rms(lat, kvn_ref[...])
    kr = _rope_slot(jnp.dot(xn, wkr_ref[...], preferred_element_type=F32), cos, s_hi, s_lo)
    kr_ref[...] = pltpu.roll(kr, LANES - ROPE_LO, axis=1)[:, :MLA_ROPE]


def _mla_pre(h, g_pre, wq_a, q_norm, wq_b, w_lat, kv_norm, w_kr, cos, s_hi, s_lo):
    b, l, _ = h.shape
    tm = _row_tile(l, 512)
    row = lambda i, j: (i, j, 0)
    tab = pl.BlockSpec((tm, LANES), lambda i, j: (j, 0))
    return pl.pallas_call(
        _mla_pre_kernel,
        grid=(b, l // tm),
        in_specs=[
            pl.BlockSpec((None, tm, D_MODEL), row),
            _resident((1, D_MODEL)),
            _resident((D_MODEL, MLA_Q_LORA)),
            _resident((1, MLA_Q_LORA)),
            _resident((MLA_Q_LORA, MLA_Q_WIDTH)),
            _resident((D_MODEL, MLA_KV_LORA)),
            _resident((1, MLA_KV_LORA)),
            _resident((D_MODEL, LANES)),
            tab, tab, tab,
        ],
        out_specs=[
            pl.BlockSpec((None, tm, MLA_Q_WIDTH), row),
            pl.BlockSpec((None, tm, MLA_KV_LORA), row),
            pl.BlockSpec((None, tm, MLA_ROPE), row),
        ],
        out_shape=[
            jax.ShapeDtypeStruct((b, l, MLA_Q_WIDTH), BF16),
            jax.ShapeDtypeStruct((b, l, MLA_KV_LORA), F32),
            jax.ShapeDtypeStruct((b, l, MLA_ROPE), F32),
        ],
        compiler_params=_params("parallel", "parallel"),
        name="mla_pre",
    )(h, g_pre, wq_a, q_norm, wq_b, w_lat, kv_norm, w_kr, cos, s_hi, s_lo)


def _kv_expand_kernel(lat_ref, kr_ref, wk_ref, wv_ref, place_ref, k_ref, v_ref):
    lat = lat_ref[...].astype(BF16)
    kr_slot = jnp.dot(kr_ref[...].astype(BF16), place_ref[...], preferred_element_type=F32)
    ones_lane = (lax.broadcasted_iota(jnp.int32, (1, HEAD_SLOT), 1) == MLA_V).astype(F32)
    kr_pair = jnp.concatenate([kr_slot, kr_slot], axis=1)
    ones_pair = jnp.concatenate([ones_lane, ones_lane], axis=1)
    for c0 in range(0, MLA_Q_WIDTH, MXU_WIDTH):
        sl = slice(c0, c0 + MXU_WIDTH)
        kb = jnp.dot(lat, wk_ref[:, sl], preferred_element_type=F32)
        k_ref[:, sl] = (kb + kr_pair).astype(BF16)
        vb = jnp.dot(lat, wv_ref[:, sl], preferred_element_type=F32)
        v_ref[:, sl] = (vb + ones_pair).astype(BF16)


def _kv_expand(lat, kr, w_k, w_v, place):
    t = lat.shape[0]
    tm = _row_tile(t, 512)
    return pl.pallas_call(
        _kv_expand_kernel,
        grid=(t // tm,),
        in_specs=[
            pl.BlockSpec((tm, MLA_KV_LORA), lambda i: (i, 0)),
            pl.BlockSpec((tm, MLA_ROPE), lambda i: (i, 0)),
            _resident((MLA_KV_LORA, MLA_Q_WIDTH)),
            _resident((MLA_KV_LORA, MLA_Q_WIDTH)),
            _resident((MLA_ROPE, LANES)),
        ],
        out_specs=[
            pl.BlockSpec((tm, MLA_Q_WIDTH), lambda i: (i, 0)),
            pl.BlockSpec((tm, MLA_Q_WIDTH), lambda i: (i, 0)),
        ],
        out_shape=[
            jax.ShapeDtypeStruct((t, MLA_Q_WIDTH), BF16),
            jax.ShapeDtypeStruct((t, MLA_Q_WIDTH), BF16),
        ],
        compiler_params=_params("parallel"),
        name="mla_kv_expand",
    )(lat, kr, w_k, w_v, place)


ATTN_SUB_K = 256
ATTN_SMALL_Q = 64
ATTN_FULL, ATTN_MASKED, ATTN_DIAGONAL = 0, 1, 2


def _kv_tiles_needed(qi, *, tq, tk, q_off, lk):
    last_chunk = (q_off + qi * tq + tq - 1) // CHUNK
    k_end = min((last_chunk + 1) * CHUNK, lk)
    return (k_end + tk - 1) // tk


def _attn_kernel(qi_ref, ki_ref, last_ref, mode_ref, q_ref, k_ref, v_ref, o_ref, m_ref, acc_ref,
                 *, tq, tk, sub_k, q_off, lk):
    step_id = pl.program_id(1)
    qi = qi_ref[step_id]
    ki = ki_ref[step_id]

    @pl.when(ki == 0)
    def _():
        m_ref[...] = jnp.full(m_ref.shape, MASK_VALUE, F32)
        acc_ref[...] = jnp.zeros(acc_ref.shape, F32)

    def step(mode):
        masked = mode != ATTN_FULL
        if masked:
            q_chunk = (q_off + qi * tq + lax.broadcasted_iota(jnp.int32, (tq, tk), 0)) // CHUNK
            k_pos = ki * tk + lax.broadcasted_iota(jnp.int32, (tq, tk), 1)
            visible = jnp.logical_and(k_pos // CHUNK <= q_chunk, k_pos < lk)
        for hd in range(MLA_HEADS):
            sl = slice(hd * HEAD_SLOT, (hd + 1) * HEAD_SLOT)
            for k0 in range(0, tk, sub_k):
                ks = slice(k0, k0 + sub_k)
                rs = slice(k0 if mode == ATTN_DIAGONAL else 0, tq)
                s = lax.dot_general(q_ref[rs, sl], k_ref[ks, sl], (((1,), (1,)), ((), ())),
                                    preferred_element_type=F32)
                if masked:
                    s = jnp.where(visible[rs, ks], s, MASK_VALUE)
                m_prev = m_ref[hd, rs, :]
                m_new = jnp.maximum(m_prev, jnp.max(s, axis=-1, keepdims=True))
                alpha = jnp.exp2(m_prev - m_new)
                p = jnp.exp2((s - jnp.tile(m_new, (1, sub_k // LANES))).astype(BF16))
                m_ref[hd, rs, :] = m_new
                pv = jnp.dot(p, v_ref[ks, sl], preferred_element_type=F32)
                acc_ref[hd, rs, :] = alpha * acc_ref[hd, rs, :] + pv

    for mode in (ATTN_FULL, ATTN_MASKED, ATTN_DIAGONAL):
        pl.when(mode_ref[step_id] == mode)(functools.partial(step, mode))

    @pl.when(last_ref[step_id] == 1)
    def _():
        low_half = lax.broadcasted_iota(jnp.int32, (tq, LANES), 1) < MLA_V
        for pair in range(MLA_HEADS // 2):
            even = acc_ref[2 * pair]
            odd = acc_ref[2 * pair + 1]
            even = even / even[:, MLA_V:MLA_V + 1]
            odd = odd / odd[:, MLA_V:MLA_V + 1]
            packed = jnp.where(low_half, even, pltpu.roll(odd, MLA_V, axis=1))
            o_ref[:, pair * LANES:(pair + 1) * LANES] = packed.astype(BF16)


def _attention(q, k, v, *, q_off, lk):
    b, lq, _ = q.shape
    lk_pad = k.shape[1]
    tq = _row_tile(lq, 512)
    tk = lk_pad if lk_pad <= 1536 else 512
    assert lk_pad % tk == 0 and tk % LANES == 0
    static = dict(tq=tq, tk=tk, q_off=q_off, lk=lk)

    sub_k = ATTN_SUB_K if tq > ATTN_SMALL_Q else tk

    def tile_mode(qi, ki):
        q_start, k_start = q_off + qi * tq, ki * tk
        inside = k_start + tk <= lk
        if inside and (k_start + tk - 1) // CHUNK <= q_start // CHUNK:
            return ATTN_FULL
        if inside and k_start == q_start and tq == tk and q_start % CHUNK == 0 and sub_k % CHUNK == 0:
            return ATTN_DIAGONAL
        return ATTN_MASKED

    qi_tab, ki_tab, last_tab, mode_tab = [], [], [], []
    for qi in range(lq // tq):
        needed = _kv_tiles_needed(qi, **static)
        qi_tab += [qi] * needed
        ki_tab += list(range(needed))
        last_tab += [0] * (needed - 1) + [1]
        mode_tab += [tile_mode(qi, ki) for ki in range(needed)]
    tables = [jnp.asarray(tab, jnp.int32) for tab in (qi_tab, ki_tab, last_tab, mode_tab)]

    q_index = lambda i, s, qi_ref, ki_ref, last_ref, mode_ref: (i, qi_ref[s], 0)
    k_index = lambda i, s, qi_ref, ki_ref, last_ref, mode_ref: (i, ki_ref[s], 0)
    return pl.pallas_call(
        functools.partial(_attn_kernel, sub_k=sub_k, **static),
        grid_spec=pltpu.PrefetchScalarGridSpec(
            num_scalar_prefetch=4,
            grid=(b, len(qi_tab)),
            in_specs=[
                pl.BlockSpec((None, tq, MLA_Q_WIDTH), q_index),
                pl.BlockSpec((None, tk, MLA_Q_WIDTH), k_index),
                pl.BlockSpec((None, tk, MLA_Q_WIDTH), k_index),
            ],
            out_specs=pl.BlockSpec((None, tq, MLA_V_WIDTH), q_index),
            scratch_shapes=[
                pltpu.VMEM((MLA_HEADS, tq, LANES), F32),
                pltpu.VMEM((MLA_HEADS, tq, LANES), F32),
            ],
        ),
        out_shape=jax.ShapeDtypeStruct((b, lq, MLA_V_WIDTH), BF16),
        compiler_params=_params("parallel", "arbitrary"),
        name="mla_attention",
    )(*tables, q, k, v)


def _prep_params(ln_mix_pre, ln_mix_post, ln_ffn_pre, ln_ffn_post,
                 ssd_w_in, ssd_conv_w, ssd_conv_b, ssd_dt_bias, ssd_a_log, ssd_d, ssd_gate_norm, ssd_w_out,
                 mla_wq_a, mla_q_norm, mla_wq_b, mla_wkv_a, mla_kv_norm, mla_w_uk, mla_w_uv, mla_w_o,
                 ffn_w_gate, ffn_w_up, ffn_w_down):
    row = lambda v: v.reshape(1, -1).astype(F32)
    pad_lanes = lambda v: jnp.pad(v, ((0, 0), (0, LANES - v.shape[1])))
    head_slots = lambda w, d: jnp.pad(
        w.reshape(w.shape[0], MLA_HEADS, d), ((0, 0), (0, 0), (0, HEAD_SLOT - d))).reshape(w.shape[0], -1)
    w_in = ssd_w_in[0]
    w_kr = jnp.zeros((D_MODEL, LANES), F32).at[:, ROPE_LO:ROPE_LO + MLA_ROPE].set(mla_wkv_a[0][:, MLA_KV_LORA:])
    place = jnp.zeros((MLA_ROPE, LANES), F32).at[jnp.arange(MLA_ROPE), ROPE_LO + jnp.arange(MLA_ROPE)].set(1.0)
    return dict(
        ln_mix_pre=[row(ln_mix_pre[i]) for i in range(2)],
        ln_mix_post=[row(ln_mix_post[i]) for i in range(2)],
        ln_ffn_pre=[row(ln_ffn_pre[i]) for i in range(2)],
        ln_ffn_post=[row(ln_ffn_post[i]) for i in range(2)],
        w_zx=w_in[:, :SSD_ZX_DIM].astype(BF16),
        w_dt=pad_lanes(w_in[:, SSD_ZX_DIM:]).astype(BF16),
        conv_w=ssd_conv_w[0].astype(F32),
        conv_b=row(ssd_conv_b[0]),
        dt_bias=pad_lanes(row(ssd_dt_bias[0])),
        a_log=pad_lanes(row(ssd_a_log[0])),
        d_skip=row(jnp.repeat(ssd_d[0], SSD_HEAD_DIM)),
        gate_norm=row(ssd_gate_norm[0]),
        w_out=ssd_w_out[0].astype(BF16),
        wq_a=mla_wq_a[0].astype(BF16),
        q_norm=row(mla_q_norm[0]),
        wq_b=head_slots(mla_wq_b[0], MLA_QK).astype(BF16),
        w_lat=mla_wkv_a[0][:, :MLA_KV_LORA].astype(BF16),
        kv_norm=row(mla_kv_norm[0]),
        w_kr=w_kr.astype(BF16),
        w_k=head_slots(mla_w_uk[0].reshape(MLA_KV_LORA, -1), MLA_NOPE).astype(BF16),
        w_v=head_slots(mla_w_uv[0].reshape(MLA_KV_LORA, -1), MLA_V).astype(BF16),
        place=place.astype(BF16),
        w_o=mla_w_o[0].astype(BF16),
        ffn_w_gate=[ffn_w_gate[i].astype(BF16) for i in range(2)],
        ffn_w_up=[ffn_w_up[i].astype(BF16) for i in range(2)],
        ffn_w_down=[ffn_w_down[i].astype(BF16) for i in range(2)],
    )


def _rope_tables(pos):
    inv = ROPE_THETA ** (-jnp.arange(MLA_ROPE_HALF, dtype=F32) / MLA_ROPE_HALF)
    ang = pos.astype(F32)[:, None] * inv[None, :]
    cos, sin = jnp.cos(ang), jnp.sin(ang)
    l = pos.shape[0]
    ones = jnp.ones((l, ROPE_LO), F32)
    tail = jnp.ones((l, LANES - ROPE_LO - MLA_ROPE), F32)
    zeros_lo = jnp.zeros((l, ROPE_LO), F32)
    zeros_half = jnp.zeros((l, MLA_ROPE_HALF), F32)
    zeros_tail = jnp.zeros((l, LANES - ROPE_LO - MLA_ROPE), F32)
    cos_t = jnp.concatenate([ones, cos, cos, tail], axis=1)
    s_hi = jnp.concatenate([zeros_lo, zeros_half, sin, zeros_tail], axis=1)
    s_lo = jnp.concatenate([zeros_lo, -sin, zeros_half, zeros_tail], axis=1)
    return cos_t, s_hi, s_lo


def _round_up(n, m):
    return -(-n // m) * m


def _trunk(x, conv_init, state_init, past_lat, past_kr, p):
    b, l, _ = x.shape
    t = b * l
    past = past_lat.shape[1]
    flat = lambda a: a.reshape(t, a.shape[-1])

    zx, dt_raw = _inproj(flat(x), p['ln_mix_pre'][0], p['w_zx'], p['w_dt'])
    l_pad = _round_up(l, SSD_Q)
    seq = lambda a: jnp.pad(a.reshape(b, l, a.shape[-1]), ((0, 0), (0, l_pad - l), (0, 0)))
    h1, new_conv, new_state = _ssd(
        seq(zx), seq(dt_raw), seq(flat(x)), conv_init, state_init.reshape(b, -1, SSD_STATE),
        p['conv_w'], p['conv_b'], p['dt_bias'], p['a_log'], p['d_skip'], p['gate_norm'],
        p['w_out'], p['ln_mix_post'][0], l_valid=l)
    h1 = flat(h1[:, :l])
    h2 = _ffn(h1, p['ln_ffn_pre'][0], p['ffn_w_gate'][0], p['ffn_w_up'][0], p['ffn_w_down'][0],
              p['ln_ffn_post'][0])

    cos, s_hi, s_lo = _rope_tables(past + jnp.arange(l, dtype=jnp.int32))
    q, lat, kr = _mla_pre(h2.reshape(b, l, D_MODEL), p['ln_mix_pre'][1], p['wq_a'], p['q_norm'], p['wq_b'],
                          p['w_lat'], p['kv_norm'], p['w_kr'], cos, s_hi, s_lo)
    lk = past + l
    lk_pad = _round_up(lk, ATTN_SUB_K if l > ATTN_SMALL_Q else LANES)
    keys = lambda new, old: jnp.pad(jnp.concatenate([old.astype(F32), new], axis=1),
                                    ((0, 0), (0, lk_pad - lk), (0, 0)))
    k_all, v_all = _kv_expand(keys(lat, past_lat).reshape(b * lk_pad, -1),
                              keys(kr, past_kr).reshape(b * lk_pad, -1), p['w_k'], p['w_v'], p['place'])
    o = _attention(q, k_all.reshape(b, lk_pad, -1), v_all.reshape(b, lk_pad, -1), q_off=past, lk=lk)
    h4 = _ffn(h2, p['ln_ffn_pre'][1], p['ffn_w_gate'][1], p['ffn_w_up'][1], p['ffn_w_down'][1],
              p['ln_ffn_post'][1], mix=(flat(o), p['w_o'], p['ln_mix_post'][1]))

    return (h4.reshape(b, l, D_MODEL), new_conv[None],
            new_state.reshape(b, SSD_HEADS, SSD_HEAD_DIM, SSD_STATE)[None], lat[None], kr[None])


def kernel(x_prompt, x_sample, state_ssd_conv, state_ssd_ssm, cache_mla_latent, cache_mla_krope,
           ln_mix_pre, ln_mix_post, ln_ffn_pre, ln_ffn_post,
           ssd_w_in, ssd_conv_w, ssd_conv_b, ssd_dt_bias, ssd_a_log, ssd_d, ssd_gate_norm, ssd_w_out,
           mla_wq_a, mla_q_norm, mla_wq_b, mla_wkv_a, mla_kv_norm, mla_w_uk, mla_w_uv, mla_w_o,
           ffn_w_gate, ffn_w_up, ffn_w_down):
    p = _prep_params(ln_mix_pre, ln_mix_post, ln_ffn_pre, ln_ffn_post,
                     ssd_w_in, ssd_conv_w, ssd_conv_b, ssd_dt_bias, ssd_a_log, ssd_d, ssd_gate_norm,
                     ssd_w_out, mla_wq_a, mla_q_norm, mla_wq_b, mla_wkv_a, mla_kv_norm, mla_w_uk,
                     mla_w_uv, mla_w_o, ffn_w_gate, ffn_w_up, ffn_w_down)
    b = x_prompt.shape[0]
    y_p, p_conv, p_ssm, p_lat, p_kr = _trunk(
        x_prompt,
        jnp.zeros((b, CONV_TAIL, SSD_CONV_DIM), F32),
        jnp.zeros((b, SSD_HEADS, SSD_HEAD_DIM, SSD_STATE), F32),
        jnp.zeros((b, 0, MLA_KV_LORA), F32),
        jnp.zeros((b, 0, MLA_ROPE), F32), p)
    y_s, s_conv, s_ssm, s_lat, s_kr = _trunk(
        x_sample, state_ssd_conv[0], state_ssd_ssm[0], cache_mla_latent[0], cache_mla_krope[0], p)
    return (y_p, y_s, p_conv, p_ssm, p_lat, p_kr, s_conv, s_ssm, s_lat, s_kr)
```

```python
import functools
import math

import jax
import jax.numpy as jnp
from jax import lax
from jax.experimental import pallas as pl
from jax.experimental.pallas import tpu as pltpu

F32 = jnp.float32
BF16 = jnp.bfloat16

D_MODEL = 1024
NORM_EPS = 1e-6
CHUNK = 64

SSD_D_INNER = 2048
SSD_HEAD_DIM = 64
SSD_HEADS = 32
SSD_GROUPS = 8
SSD_HEADS_PER_GROUP = 4
SSD_STATE = 128
SSD_CONV = 4
SSD_GN = SSD_GROUPS * SSD_STATE
SSD_CONV_DIM = SSD_D_INNER + 2 * SSD_GN
SSD_ZX_DIM = SSD_D_INNER + SSD_CONV_DIM
SSD_GROUP_WIDTH = SSD_HEADS_PER_GROUP * SSD_HEAD_DIM

MLA_HEADS = 16
MLA_Q_LORA = 512
MLA_KV_LORA = 256
MLA_NOPE = 64
MLA_ROPE = 32
MLA_ROPE_HALF = MLA_ROPE // 2
MLA_QK = MLA_NOPE + MLA_ROPE
MLA_V = 64
MLA_SCALE = 1.0 / math.sqrt(MLA_QK)
MLA_SCALE_LOG2 = MLA_SCALE * math.log2(math.e)
ROPE_THETA = 10000.0

FFN_HIDDEN = 2816

LANES = 128
MXU_WIDTH = 256
VMEM_LIMIT_BYTES = 56 * 1024 * 1024

HEAD_SLOT = LANES
ROPE_LO = MLA_NOPE
ROPE_HI = MLA_NOPE + MLA_ROPE_HALF
MLA_Q_WIDTH = MLA_HEADS * HEAD_SLOT
MLA_V_WIDTH = MLA_HEADS * MLA_V
MASK_VALUE = -1e30


def _rms(x, g):
    return x * lax.rsqrt(jnp.mean(x * x, axis=-1, keepdims=True) + NORM_EPS) * g


def _silu_of_twice(half):
    return half + half * jnp.tanh(half)


def _silu(x):
    return _silu_of_twice(0.5 * x)


def _resident(shape):
    zeros = (0,) * len(shape)
    return pl.BlockSpec(shape, lambda *_: zeros, pipeline_mode=pl.Buffered(1))


def _params(*semantics):
    return pltpu.CompilerParams(dimension_semantics=semantics, vmem_limit_bytes=VMEM_LIMIT_BYTES)


def _row_tile(n, target):
    if n <= target:
        return n
    for t in range(target, 0, -16):
        if n % t == 0:
            return t
    raise ValueError((n, target))


INPROJ_COLS = 512


def _inproj_kernel(x_ref, g_ref, w_ref, wdt_ref, zx_ref, dt_ref):
    xn = _rms(x_ref[...], g_ref[...]).astype(BF16)
    for n in range(0, SSD_ZX_DIM, INPROJ_COLS):
        zx_ref[:, n:n + INPROJ_COLS] = jnp.dot(
            xn, w_ref[:, n:n + INPROJ_COLS], preferred_element_type=F32).astype(BF16)
    dt_ref[...] = jnp.dot(xn, wdt_ref[...], preferred_element_type=F32)


def _inproj(x, g, w_zx, w_dt):
    t = x.shape[0]
    tm = _row_tile(t, 512)
    return pl.pallas_call(
        _inproj_kernel,
        grid=(t // tm,),
        in_specs=[
            pl.BlockSpec((tm, D_MODEL), lambda i: (i, 0)),
            _resident((1, D_MODEL)),
            _resident((D_MODEL, SSD_ZX_DIM)),
            _resident((D_MODEL, LANES)),
        ],
        out_specs=[
            pl.BlockSpec((tm, SSD_ZX_DIM), lambda i: (i, 0)),
            pl.BlockSpec((tm, LANES), lambda i: (i, 0)),
        ],
        out_shape=[
            jax.ShapeDtypeStruct((t, SSD_ZX_DIM), BF16),
            jax.ShapeDtypeStruct((t, LANES), F32),
        ],
        compiler_params=_params("parallel"),
        name="ssd_inproj",
    )(x, g, w_zx, w_dt)


CONV_TAIL = SSD_CONV - 1
CONV_PAD = 8
CONV_GROUP = 16
CONV_COLS = 512
SSD_Q = LANES


def _ssd_kernel(zx_ref, dt_ref, h_ref, cinit_ref, sinit_ref, convw_ref, convb_ref, dtb_ref,
                alog_ref, dskip_ref, gn_ref, wout_ref, gpost_ref,
                hout_ref, cout_ref, sout_ref,
                tail_ref, head_ref, xc_ref, st_ref, y_ref, act_ref,
                *, tl, l_valid):
    t = pl.program_id(1)
    nt = pl.num_programs(1)
    q = SSD_Q
    gw = SSD_GROUP_WIDTH

    @pl.when(t == 0)
    def _():
        tail_ref[...] = jnp.zeros((CONV_PAD, SSD_CONV_DIM), F32)
        tail_ref[CONV_PAD - CONV_TAIL:CONV_PAD, :] = cinit_ref[...]
        for g in range(SSD_GROUPS):
            st_ref[g] = sinit_ref[g * gw:(g + 1) * gw, :].T

    srow = lax.broadcasted_iota(jnp.int32, (tl, tl), 0)
    scol = lax.broadcasted_iota(jnp.int32, (tl, tl), 1)
    shifts = [(srow - scol == j).astype(BF16) for j in range(1, SSD_CONV)]
    for c0 in range(0, SSD_CONV_DIM, CONV_COLS):
        cs = slice(c0, c0 + CONV_COLS)
        xb = zx_ref[:, SSD_D_INNER + c0:SSD_D_INNER + c0 + CONV_COLS]
        x = xb.astype(F32)
        w_half = 0.5 * convw_ref[:, cs]
        b_half = 0.5 * convb_ref[:, cs]
        acc = b_half + x * w_half[SSD_CONV - 1:SSD_CONV]
        for j in range(1, SSD_CONV):
            shifted = jnp.dot(shifts[j - 1], xb, preferred_element_type=F32)
            acc = acc + shifted * w_half[SSD_CONV - 1 - j:SSD_CONV - j]
        xc_ref[:, cs] = _silu_of_twice(acc)

        head_ref[0:CONV_PAD, :] = tail_ref[:, cs]
        head_ref[CONV_PAD:2 * CONV_PAD, :] = x[0:CONV_PAD]
        head = b_half
        for k in range(SSD_CONV):
            lo = CONV_PAD - CONV_TAIL + k
            head = head + head_ref[lo:lo + CONV_PAD, :] * w_half[k:k + 1]
        xc_ref[0:CONV_PAD, cs] = _silu_of_twice(head)
        tail_ref[:, cs] = x[tl - CONV_PAD:tl]

    a_row = -jnp.exp(alog_ref[...])
    rows = lax.broadcasted_iota(jnp.int32, (q, q), 0)
    cols = lax.broadcasted_iota(jnp.int32, (q, q), 1)
    causal = rows >= cols
    tri = causal.astype(F32)
    low_half = lax.broadcasted_iota(jnp.int32, (q, LANES), 1) < SSD_HEAD_DIM
    head_of_lane = lax.broadcasted_iota(jnp.int32, (q, gw), 1) // SSD_HEAD_DIM
    head_masks = [head_of_lane == r for r in range(SSD_HEADS_PER_GROUP)]

    def per_head_lanes(parts):
        return jnp.concatenate([jnp.where(low_half, parts[0], parts[1]),
                                jnp.where(low_half, parts[2], parts[3])], axis=1)

    def chunk(c):
        r0 = c * q
        rs = slice(r0, r0 + q)
        raw = dt_ref[rs, :] + dtb_ref[...]
        dt = jnp.maximum(raw, 0.0) + jnp.log(1.0 + jnp.exp(-jnp.abs(raw)))
        if l_valid is not None:
            row_id = t * tl + r0 + lax.broadcasted_iota(jnp.int32, (q, LANES), 0)
            dt = jnp.where(row_id < l_valid, dt, 0.0)
        da = dt * a_row
        acum = jnp.dot(tri, da, preferred_element_type=F32, precision=lax.Precision.HIGHEST)
        act_ref[c] = acum.T

        for g in range(SSD_GROUPS):
            b0 = SSD_D_INNER + g * SSD_STATE
            c0 = SSD_D_INNER + SSD_GN + g * SSD_STATE
            gs = slice(g * gw, (g + 1) * gw)
            bg = xc_ref[rs, b0:b0 + SSD_STATE].astype(BF16)
            cg = xc_ref[rs, c0:c0 + SSD_STATE].astype(BF16)
            xg = xc_ref[rs, gs]
            cb = lax.dot_general(cg, bg, (((1,), (1,)), ((), ())), preferred_element_type=F32)
            st = st_ref[g]
            y_state = jnp.dot(cg, st.astype(BF16), preferred_element_type=F32)

            heads = range(g * SSD_HEADS_PER_GROUP, (g + 1) * SSD_HEADS_PER_GROUP)
            acols = [jnp.broadcast_to(acum[:, h:h + 1], (q, LANES)) for h in heads]
            acol_g = per_head_lanes(acols)
            dcol_g = per_head_lanes([jnp.broadcast_to(dt[:, h:h + 1], (q, LANES)) for h in heads])
            xdt = xg * dcol_g

            y = jnp.exp(acol_g) * y_state + dskip_ref[:, gs] * xg
            for r, h in enumerate(heads):
                seg = jnp.where(causal, acols[r] - act_ref[c, h:h + 1, :], MASK_VALUE)
                m = (cb * jnp.exp(seg)).astype(BF16)
                own = jnp.where(head_masks[r], xdt, 0.0).astype(BF16)
                y = y + jnp.dot(m, own, preferred_element_type=F32)

            a_last = acol_g[q - 1:q, :]
            xw = (xdt * jnp.exp(a_last - acol_g)).astype(BF16)
            upd = lax.dot_general(bg, xw, (((0,), (0,)), ((), ())), preferred_element_type=F32)
            st_ref[g] = jnp.exp(a_last) * st + upd

            zg = zx_ref[rs, gs].astype(F32)
            y_ref[rs, gs] = _rms(y * _silu(zg), gn_ref[:, gs]).astype(BF16)

    for c in range(tl // q):
        chunk(c)

    mix = jnp.dot(y_ref[...], wout_ref[...], preferred_element_type=F32)
    hout_ref[...] = h_ref[...] + _rms(mix, gpost_ref[...])

    @pl.when(t == nt - 1)
    def _():
        n_rows = tl if l_valid is None else (l_valid - 1) % tl + 1
        group = zx_ref[n_rows - CONV_GROUP:n_rows, SSD_D_INNER:].astype(F32)
        cout_ref[...] = group[CONV_GROUP - CONV_TAIL:CONV_GROUP]
        for g in range(SSD_GROUPS):
            sout_ref[g * gw:(g + 1) * gw, :] = st_ref[g].T


def _ssd(zx, dt_raw, h, conv_init, state_init, conv_w, conv_b, dt_bias, a_log, d_skip, gate_norm,
         w_out, g_post, *, l_valid):
    b, l, _ = zx.shape
    tl = _row_tile(l, 256)
    assert tl % SSD_Q == 0 and l - l_valid < tl
    hp = SSD_HEADS * SSD_HEAD_DIM
    kern = functools.partial(_ssd_kernel, tl=tl, l_valid=None if l_valid == l else l_valid)
    return pl.pallas_call(
        kern,
        grid=(b, l // tl),
        in_specs=[
            pl.BlockSpec((None, tl, SSD_ZX_DIM), lambda i, j: (i, j, 0)),
            pl.BlockSpec((None, tl, LANES), lambda i, j: (i, j, 0)),
            pl.BlockSpec((None, tl, D_MODEL), lambda i, j: (i, j, 0)),
            pl.BlockSpec((None, CONV_TAIL, SSD_CONV_DIM), lambda i, j: (i, 0, 0)),
            pl.BlockSpec((None, hp, SSD_STATE), lambda i, j: (i, 0, 0)),
            _resident((SSD_CONV, SSD_CONV_DIM)),
            _resident((1, SSD_CONV_DIM)),
            _resident((1, LANES)),
            _resident((1, LANES)),
            _resident((1, SSD_D_INNER)),
            _resident((1, SSD_D_INNER)),
            _resident((SSD_D_INNER, D_MODEL)),
            _resident((1, D_MODEL)),
        ],
        out_specs=[
            pl.BlockSpec((None, tl, D_MODEL), lambda i, j: (i, j, 0)),
            pl.BlockSpec((None, CONV_TAIL, SSD_CONV_DIM), lambda i, j: (i, 0, 0)),
            pl.BlockSpec((None, hp, SSD_STATE), lambda i, j: (i, 0, 0)),
        ],
        out_shape=[
            jax.ShapeDtypeStruct((b, l, D_MODEL), F32),
            jax.ShapeDtypeStruct((b, CONV_TAIL, SSD_CONV_DIM), F32),
            jax.ShapeDtypeStruct((b, hp, SSD_STATE), F32),
        ],
        scratch_shapes=[
            pltpu.VMEM((CONV_PAD, SSD_CONV_DIM), F32),
            pltpu.VMEM((2 * CONV_PAD, CONV_COLS), F32),
            pltpu.VMEM((tl, SSD_CONV_DIM), F32),
            pltpu.VMEM((SSD_GROUPS, SSD_STATE, SSD_GROUP_WIDTH), F32),
            pltpu.VMEM((tl, SSD_D_INNER), BF16),
            pltpu.VMEM((tl // SSD_Q, LANES, SSD_Q), F32),
        ],
        compiler_params=_params("parallel", "arbitrary"),
        name="ssd_scan",
    )(zx, dt_raw, h, conv_init, state_init, conv_w, conv_b, dt_bias, a_log, d_skip, gate_norm,
      w_out, g_post)


FFN_COLS = MXU_WIDTH


def _ffn_kernel(*refs, mixed):
    if mixed:
        a_ref, wmix_ref, gmix_ref, h_ref, gpre_ref, wg_ref, wu_ref, wd_ref, gpost_ref, out_ref = refs
        mix = jnp.dot(a_ref[...], wmix_ref[...], preferred_element_type=F32)
        h = h_ref[...] + _rms(mix, gmix_ref[...])
    else:
        h_ref, gpre_ref, wg_ref, wu_ref, wd_ref, gpost_ref, out_ref = refs
        h = h_ref[...]
    xn = _rms(h, gpre_ref[...]).astype(BF16)
    acc = None
    for f in range(0, FFN_HIDDEN, FFN_COLS):
        gate = jnp.dot(xn, wg_ref[:, f:f + FFN_COLS], preferred_element_type=F32)
        up = jnp.dot(xn, wu_ref[:, f:f + FFN_COLS], preferred_element_type=F32)
        act = (_silu(gate) * up).astype(BF16)
        part = jnp.dot(act, wd_ref[f:f + FFN_COLS, :], preferred_element_type=F32)
        acc = part if acc is None else acc + part
    out_ref[...] = h + _rms(acc, gpost_ref[...])


def _ffn(h, g_pre, w_gate, w_up, w_down, g_post, mix=None):
    t = h.shape[0]
    tm = _row_tile(t, 512)
    rows = lambda width: pl.BlockSpec((tm, width), lambda i: (i, 0))
    mix_specs, mix_args = [], []
    if mix is not None:
        a, w_mix, g_mix = mix
        mix_specs = [rows(a.shape[1]), _resident(w_mix.shape), _resident((1, D_MODEL))]
        mix_args = [a, w_mix, g_mix]
    return pl.pallas_call(
        functools.partial(_ffn_kernel, mixed=mix is not None),
        grid=(t // tm,),
        in_specs=mix_specs + [
            rows(D_MODEL),
            _resident((1, D_MODEL)),
            _resident((D_MODEL, FFN_HIDDEN)),
            _resident((D_MODEL, FFN_HIDDEN)),
            _resident((FFN_HIDDEN, D_MODEL)),
            _resident((1, D_MODEL)),
        ],
        out_specs=rows(D_MODEL),
        out_shape=jax.ShapeDtypeStruct((t, D_MODEL), F32),
        compiler_params=_params("parallel"),
        name="mix_ffn" if mix is not None else "ffn",
    )(*mix_args, h, g_pre, w_gate, w_up, w_down, g_post)


def _rope_slot(x, cos, sin_hi, sin_lo):
    return (x * cos + pltpu.roll(x, MLA_ROPE_HALF, axis=1) * sin_hi
            + pltpu.roll(x, LANES - MLA_ROPE_HALF, axis=1) * sin_lo)


def _mla_pre_kernel(h_ref, gpre_ref, wqa_ref, qn_ref, wqb_ref, wlat_ref, kvn_ref, wkr_ref,
                    cos_ref, shi_ref, slo_ref, q_ref, lat_ref, kr_ref):
    xn = _rms(h_ref[...], gpre_ref[...]).astype(BF16)
    cos, s_hi, s_lo = cos_ref[...], shi_ref[...], slo_ref[...]

    qa = jnp.dot(xn, wqa_ref[...], preferred_element_type=F32)
    qa = _rms(qa, qn_ref[...]).astype(BF16)
    for c0 in range(0, MLA_Q_WIDTH, MXU_WIDTH):
        qb = jnp.dot(qa, wqb_ref[:, c0:c0 + MXU_WIDTH], preferred_element_type=F32)
        for s0 in range(0, MXU_WIDTH, HEAD_SLOT):
            qh = _rope_slot(qb[:, s0:s0 + HEAD_SLOT], cos, s_hi, s_lo) * MLA_SCALE_LOG2
            q_ref[:, c0 + s0:c0 + s0 + HEAD_SLOT] = qh.astype(BF16)

    lat = jnp.dot(xn, wlat_ref[...], preferred_element_type=F32)
    lat_ref[...] = _rms(lat, kvn_ref[...])
    kr = _rope_slot(jnp.dot(xn, wkr_ref[...], preferred_element_type=F32), cos, s_hi, s_lo)
    kr_ref[...] = pltpu.roll(kr, LANES - ROPE_LO, axis=1)[:, :MLA_ROPE]


def _mla_pre(h, g_pre, wq_a, q_norm, wq_b, w_lat, kv_norm, w_kr, cos, s_hi, s_lo):
    b, l, _ = h.shape
    tm = _row_tile(l, 512)
    row = lambda i, j: (i, j, 0)
    tab = pl.BlockSpec((tm, LANES), lambda i, j: (j, 0))
    return pl.pallas_call(
        _mla_pre_kernel,
        grid=(b, l // tm),
        in_specs=[
            pl.BlockSpec((None, tm, D_MODEL), row),
            _resident((1, D_MODEL)),
            _resident((D_MODEL, MLA_Q_LORA)),
            _resident((1, MLA_Q_LORA)),
            _resident((MLA_Q_LORA, MLA_Q_WIDTH)),
            _resident((D_MODEL, MLA_KV_LORA)),
            _resident((1, MLA_KV_LORA)),
            _resident((D_MODEL, LANES)),
            tab, tab, tab,
        ],
        out_specs=[
            pl.BlockSpec((None, tm, MLA_Q_WIDTH), row),
            pl.BlockSpec((None, tm, MLA_KV_LORA), row),
            pl.BlockSpec((None, tm, MLA_ROPE), row),
        ],
        out_shape=[
            jax.ShapeDtypeStruct((b, l, MLA_Q_WIDTH), BF16),
            jax.ShapeDtypeStruct((b, l, MLA_KV_LORA), F32),
            jax.ShapeDtypeStruct((b, l, MLA_ROPE), F32),
        ],
        compiler_params=_params("parallel", "parallel"),
        name="mla_pre",
    )(h, g_pre, wq_a, q_norm, wq_b, w_lat, kv_norm, w_kr, cos, s_hi, s_lo)


def _kv_expand_kernel(lat_ref, kr_ref, wk_ref, wv_ref, place_ref, k_ref, v_ref):
    lat = lat_ref[...].astype(BF16)
    kr_slot = jnp.dot(kr_ref[...].astype(BF16), place_ref[...], preferred_element_type=F32)
    ones_lane = (lax.broadcasted_iota(jnp.int32, (1, HEAD_SLOT), 1) == MLA_V).astype(F32)
    kr_pair = jnp.concatenate([kr_slot, kr_slot], axis=1)
    ones_pair = jnp.concatenate([ones_lane, ones_lane], axis=1)
    for c0 in range(0, MLA_Q_WIDTH, MXU_WIDTH):
        sl = slice(c0, c0 + MXU_WIDTH)
        kb = jnp.dot(lat, wk_ref[:, sl], preferred_element_type=F32)
        k_ref[:, sl] = (kb + kr_pair).astype(BF16)
        vb = jnp.dot(lat, wv_ref[:, sl], preferred_element_type=F32)
        v_ref[:, sl] = (vb + ones_pair).astype(BF16)


def _kv_expand(lat, kr, w_k, w_v, place):
    t = lat.shape[0]
    tm = _row_tile(t, 512)
    return pl.pallas_call(
        _kv_expand_kernel,
        grid=(t // tm,),
        in_specs=[
            pl.BlockSpec((tm, MLA_KV_LORA), lambda i: (i, 0)),
            pl.BlockSpec((tm, MLA_ROPE), lambda i: (i, 0)),
            _resident((MLA_KV_LORA, MLA_Q_WIDTH)),
            _resident((MLA_KV_LORA, MLA_Q_WIDTH)),
            _resident((MLA_ROPE, LANES)),
        ],
        out_specs=[
            pl.BlockSpec((tm, MLA_Q_WIDTH), lambda i: (i, 0)),
            pl.BlockSpec((tm, MLA_Q_WIDTH), lambda i: (i, 0)),
        ],
        out_shape=[
            jax.ShapeDtypeStruct((t, MLA_Q_WIDTH), BF16),
            jax.ShapeDtypeStruct((t, MLA_Q_WIDTH), BF16),
        ],
        compiler_params=_params("parallel"),
        name="mla_kv_expand",
    )(lat, kr, w_k, w_v, place)


ATTN_SUB_K = 256
ATTN_SMALL_Q = 64
ATTN_FULL, ATTN_MASKED, ATTN_DIAGONAL = 0, 1, 2


def _kv_tiles_needed(qi, *, tq, tk, q_off, lk):
    last_chunk = (q_off + qi * tq + tq - 1) // CHUNK
    k_end = min((last_chunk + 1) * CHUNK, lk)
    return (k_end + tk - 1) // tk


def _attn_kernel(qi_ref, ki_ref, last_ref, mode_ref, q_ref, k_ref, v_ref, o_ref, m_ref, acc_ref,
                 *, tq, tk, sub_k, q_off, lk, variants):
    step_id = pl.program_id(1)
    qi = qi_ref[step_id]
    ki = ki_ref[step_id]

    @pl.when(ki == 0)
    def _():
        m_ref[...] = jnp.full(m_ref.shape, MASK_VALUE, F32)
        acc_ref[...] = jnp.zeros(acc_ref.shape, F32)

    def sub_steps(kind, lead):
        for k0 in range(0, tk, sub_k):
            if kind == ATTN_FULL:
                yield k0, 0, False
            elif kind == ATTN_MASKED:
                yield k0, 0, True
            elif max(0, k0 - lead) < tq:
                yield k0, max(0, k0 - lead), k0 + sub_k > lead + CHUNK

    def step(kind, lead):
        if any(masked for _, _, masked in sub_steps(kind, lead)):
            q_chunk = (q_off + qi * tq + lax.broadcasted_iota(jnp.int32, (tq, tk), 0)) // CHUNK
            k_pos = ki * tk + lax.broadcasted_iota(jnp.int32, (tq, tk), 1)
            visible = jnp.logical_and(k_pos // CHUNK <= q_chunk, k_pos < lk)
        for hd in range(MLA_HEADS):
            sl = slice(hd * HEAD_SLOT, (hd + 1) * HEAD_SLOT)
            for k0, r0, masked in sub_steps(kind, lead):
                ks = slice(k0, k0 + sub_k)
                rs = slice(r0, tq)
                s = lax.dot_general(q_ref[rs, sl], k_ref[ks, sl], (((1,), (1,)), ((), ())),
                                    preferred_element_type=F32)
                if masked:
                    s = jnp.where(visible[rs, ks], s, MASK_VALUE)
                m_prev = m_ref[hd, rs, :]
                m_new = jnp.maximum(m_prev, jnp.max(s, axis=-1, keepdims=True))
                alpha = jnp.exp2((m_prev - m_new).astype(BF16)).astype(F32)
                p = jnp.exp2((s - jnp.tile(m_new, (1, sub_k // LANES))).astype(BF16))
                m_ref[hd, rs, :] = m_new
                pv = jnp.dot(p, v_ref[ks, sl], preferred_element_type=F32)
                acc_ref[hd, rs, :] = alpha * acc_ref[hd, rs, :] + pv

    for index, (kind, lead) in enumerate(variants):
        pl.when(mode_ref[step_id] == index)(functools.partial(step, kind, lead))

    @pl.when(last_ref[step_id] == 1)
    def _():
        low_half = lax.broadcasted_iota(jnp.int32, (tq, LANES), 1) < MLA_V
        for pair in range(MLA_HEADS // 2):
            even = acc_ref[2 * pair]
            odd = acc_ref[2 * pair + 1]
            even = even / even[:, MLA_V:MLA_V + 1]
            odd = odd / odd[:, MLA_V:MLA_V + 1]
            packed = jnp.where(low_half, even, pltpu.roll(odd, MLA_V, axis=1))
            o_ref[:, pair * LANES:(pair + 1) * LANES] = packed.astype(BF16)


def _attention(q, k, v, *, q_off, lk):
    b, lq, _ = q.shape
    lk_pad = k.shape[1]
    tq = _row_tile(lq, 512)
    tk = lk_pad if lk_pad <= 1536 else 1024
    assert lk_pad % tk == 0 and tk % LANES == 0
    static = dict(tq=tq, tk=tk, q_off=q_off, lk=lk)

    sub_k = ATTN_SUB_K if tq > ATTN_SMALL_Q else tk

    def tile_variant(qi, ki):
        q_start, k_start = q_off + qi * tq, ki * tk
        inside = k_start + tk <= lk
        if inside and (k_start + tk - 1) // CHUNK <= q_start // CHUNK:
            return (ATTN_FULL, 0)
        if inside and q_start % CHUNK == 0 and k_start % CHUNK == 0 and sub_k % CHUNK == 0:
            return (ATTN_DIAGONAL, q_start - k_start)
        return (ATTN_MASKED, 0)

    qi_tab, ki_tab, last_tab, mode_tab, variants = [], [], [], [], []
    for qi in range(lq // tq):
        needed = _kv_tiles_needed(qi, **static)
        qi_tab += [qi] * needed
        ki_tab += list(range(needed))
        last_tab += [0] * (needed - 1) + [1]
        for ki in range(needed):
            variant = tile_variant(qi, ki)
            if variant not in variants:
                variants.append(variant)
            mode_tab.append(variants.index(variant))
    tables = [jnp.asarray(tab, jnp.int32) for tab in (qi_tab, ki_tab, last_tab, mode_tab)]

    q_index = lambda i, s, qi_ref, ki_ref, last_ref, mode_ref: (i, qi_ref[s], 0)
    k_index = lambda i, s, qi_ref, ki_ref, last_ref, mode_ref: (i, ki_ref[s], 0)
    return pl.pallas_call(
        functools.partial(_attn_kernel, sub_k=sub_k, variants=tuple(variants), **static),
        grid_spec=pltpu.PrefetchScalarGridSpec(
            num_scalar_prefetch=4,
            grid=(b, len(qi_tab)),
            in_specs=[
                pl.BlockSpec((None, tq, MLA_Q_WIDTH), q_index),
                pl.BlockSpec((None, tk, MLA_Q_WIDTH), k_index),
                pl.BlockSpec((None, tk, MLA_Q_WIDTH), k_index),
            ],
            out_specs=pl.BlockSpec((None, tq, MLA_V_WIDTH), q_index),
            scratch_shapes=[
                pltpu.VMEM((MLA_HEADS, tq, LANES), F32),
                pltpu.VMEM((MLA_HEADS, tq, LANES), F32),
            ],
        ),
        out_shape=jax.ShapeDtypeStruct((b, lq, MLA_V_WIDTH), BF16),
        compiler_params=_params("parallel", "arbitrary"),
        name="mla_attention",
    )(*tables, q, k, v)


def _prep_params(ln_mix_pre, ln_mix_post, ln_ffn_pre, ln_ffn_post,
                 ssd_w_in, ssd_conv_w, ssd_conv_b, ssd_dt_bias, ssd_a_log, ssd_d, ssd_gate_norm, ssd_w_out,
                 mla_wq_a, mla_q_norm, mla_wq_b, mla_wkv_a, mla_kv_norm, mla_w_uk, mla_w_uv, mla_w_o,
                 ffn_w_gate, ffn_w_up, ffn_w_down):
    row = lambda v: v.reshape(1, -1).astype(F32)
    pad_lanes = lambda v: jnp.pad(v, ((0, 0), (0, LANES - v.shape[1])))
    head_slots = lambda w, d: jnp.pad(
        w.reshape(w.shape[0], MLA_HEADS, d), ((0, 0), (0, 0), (0, HEAD_SLOT - d))).reshape(w.shape[0], -1)
    w_in = ssd_w_in[0]
    w_kr = jnp.zeros((D_MODEL, LANES), F32).at[:, ROPE_LO:ROPE_LO + MLA_ROPE].set(mla_wkv_a[0][:, MLA_KV_LORA:])
    place = jnp.zeros((MLA_ROPE, LANES), F32).at[jnp.arange(MLA_ROPE), ROPE_LO + jnp.arange(MLA_ROPE)].set(1.0)
    return dict(
        ln_mix_pre=[row(ln_mix_pre[i]) for i in range(2)],
        ln_mix_post=[row(ln_mix_post[i]) for i in range(2)],
        ln_ffn_pre=[row(ln_ffn_pre[i]) for i in range(2)],
        ln_ffn_post=[row(ln_ffn_post[i]) for i in range(2)],
        w_zx=w_in[:, :SSD_ZX_DIM].astype(BF16),
        w_dt=pad_lanes(w_in[:, SSD_ZX_DIM:]).astype(BF16),
        conv_w=ssd_conv_w[0].astype(F32),
        conv_b=row(ssd_conv_b[0]),
        dt_bias=pad_lanes(row(ssd_dt_bias[0])),
        a_log=pad_lanes(row(ssd_a_log[0])),
        d_skip=row(jnp.repeat(ssd_d[0], SSD_HEAD_DIM)),
        gate_norm=row(ssd_gate_norm[0]),
        w_out=ssd_w_out[0].astype(BF16),
        wq_a=mla_wq_a[0].astype(BF16),
        q_norm=row(mla_q_norm[0]),
        wq_b=head_slots(mla_wq_b[0], MLA_QK).astype(BF16),
        w_lat=mla_wkv_a[0][:, :MLA_KV_LORA].astype(BF16),
        kv_norm=row(mla_kv_norm[0]),
        w_kr=w_kr.astype(BF16),
        w_k=head_slots(mla_w_uk[0].reshape(MLA_KV_LORA, -1), MLA_NOPE).astype(BF16),
        w_v=head_slots(mla_w_uv[0].reshape(MLA_KV_LORA, -1), MLA_V).astype(BF16),
        place=place.astype(BF16),
        w_o=mla_w_o[0].astype(BF16),
        ffn_w_gate=[ffn_w_gate[i].astype(BF16) for i in range(2)],
        ffn_w_up=[ffn_w_up[i].astype(BF16) for i in range(2)],
        ffn_w_down=[ffn_w_down[i].astype(BF16) for i in range(2)],
    )


def _rope_tables(pos):
    inv = ROPE_THETA ** (-jnp.arange(MLA_ROPE_HALF, dtype=F32) / MLA_ROPE_HALF)
    ang = pos.astype(F32)[:, None] * inv[None, :]
    cos, sin = jnp.cos(ang), jnp.sin(ang)
    l = pos.shape[0]
    ones = jnp.ones((l, ROPE_LO), F32)
    tail = jnp.ones((l, LANES - ROPE_LO - MLA_ROPE), F32)
    zeros_lo = jnp.zeros((l, ROPE_LO), F32)
    zeros_half = jnp.zeros((l, MLA_ROPE_HALF), F32)
    zeros_tail = jnp.zeros((l, LANES - ROPE_LO - MLA_ROPE), F32)
    cos_t = jnp.concatenate([ones, cos, cos, tail], axis=1)
    s_hi = jnp.concatenate([zeros_lo, zeros_half, sin, zeros_tail], axis=1)
    s_lo = jnp.concatenate([zeros_lo, -sin, zeros_half, zeros_tail], axis=1)
    return cos_t, s_hi, s_lo


def _round_up(n, m):
    return -(-n // m) * m


def _trunk(x, conv_init, state_init, past_lat, past_kr, p):
    b, l, _ = x.shape
    t = b * l
    past = past_lat.shape[1]
    flat = lambda a: a.reshape(t, a.shape[-1])

    zx, dt_raw = _inproj(flat(x), p['ln_mix_pre'][0], p['w_zx'], p['w_dt'])
    l_pad = _round_up(l, SSD_Q)
    seq = lambda a: jnp.pad(a.reshape(b, l, a.shape[-1]), ((0, 0), (0, l_pad - l), (0, 0)))
    h1, new_conv, new_state = _ssd(
        seq(zx), seq(dt_raw), seq(flat(x)), conv_init, state_init.reshape(b, -1, SSD_STATE),
        p['conv_w'], p['conv_b'], p['dt_bias'], p['a_log'], p['d_skip'], p['gate_norm'],
        p['w_out'], p['ln_mix_post'][0], l_valid=l)
    h1 = flat(h1[:, :l])
    h2 = _ffn(h1, p['ln_ffn_pre'][0], p['ffn_w_gate'][0], p['ffn_w_up'][0], p['ffn_w_down'][0],
              p['ln_ffn_post'][0])

    cos, s_hi, s_lo = _rope_tables(past + jnp.arange(l, dtype=jnp.int32))
    q, lat, kr = _mla_pre(h2.reshape(b, l, D_MODEL), p['ln_mix_pre'][1], p['wq_a'], p['q_norm'], p['wq_b'],
                          p['w_lat'], p['kv_norm'], p['w_kr'], cos, s_hi, s_lo)
    lk = past + l
    lk_pad = _round_up(lk, ATTN_SUB_K if l > ATTN_SMALL_Q else LANES)
    keys = lambda new, old: jnp.pad(jnp.concatenate([old.astype(F32), new], axis=1),
                                    ((0, 0), (0, lk_pad - lk), (0, 0)))
    k_all, v_all = _kv_expand(keys(lat, past_lat).reshape(b * lk_pad, -1),
                              keys(kr, past_kr).reshape(b * lk_pad, -1), p['w_k'], p['w_v'], p['place'])
    o = _attention(q, k_all.reshape(b, lk_pad, -1), v_all.reshape(b, lk_pad, -1), q_off=past, lk=lk)
    h4 = _ffn(h2, p['ln_ffn_pre'][1], p['ffn_w_gate'][1], p['ffn_w_up'][1], p['ffn_w_down'][1],
              p['ln_ffn_post'][1], mix=(flat(o), p['w_o'], p['ln_mix_post'][1]))

    return (h4.reshape(b, l, D_MODEL), new_conv[None],
            new_state.reshape(b, SSD_HEADS, SSD_HEAD_DIM, SSD_STATE)[None], lat[None], kr[None])


def kernel(x_prompt, x_sample, state_ssd_conv, state_ssd_ssm, cache_mla_latent, cache_mla_krope,
           ln_mix_pre, ln_mix_post, ln_ffn_pre, ln_ffn_post,
           ssd_w_in, ssd_conv_w, ssd_conv_b, ssd_dt_bias, ssd_a_log, ssd_d, ssd_gate_norm, ssd_w_out,
           mla_wq_a, mla_q_norm, mla_wq_b, mla_wkv_a, mla_kv_norm, mla_w_uk, mla_w_uv, mla_w_o,
           ffn_w_gate, ffn_w_up, ffn_w_down):
    p = _prep_params(ln_mix_pre, ln_mix_post, ln_ffn_pre, ln_ffn_post,
                     ssd_w_in, ssd_conv_w, ssd_conv_b, ssd_dt_bias, ssd_a_log, ssd_d, ssd_gate_norm,
                     ssd_w_out, mla_wq_a, mla_q_norm, mla_wq_b, mla_wkv_a, mla_kv_norm, mla_w_uk,
                     mla_w_uv, mla_w_o, ffn_w_gate, ffn_w_up, ffn_w_down)
    b = x_prompt.shape[0]
    y_p, p_conv, p_ssm, p_lat, p_kr = _trunk(
        x_prompt,
        jnp.zeros((b, CONV_TAIL, SSD_CONV_DIM), F32),
        jnp.zeros((b, SSD_HEADS, SSD_HEAD_DIM, SSD_STATE), F32),
        jnp.zeros((b, 0, MLA_KV_LORA), F32),
        jnp.zeros((b, 0, MLA_ROPE), F32), p)
    y_s, s_conv, s_ssm, s_lat, s_kr = _trunk(
        x_sample, state_ssd_conv[0], state_ssd_ssm[0], cache_mla_latent[0], cache_mla_krope[0], p)
    return (y_p, y_s, p_conv, p_ssm, p_lat, p_kr, s_conv, s_ssm, s_lat, s_kr)
```

```python
import functools
import math

import jax
import jax.numpy as jnp
from jax import lax
from jax.experimental import pallas as pl
from jax.experimental.pallas import tpu as pltpu

F32 = jnp.float32
BF16 = jnp.bfloat16

D_MODEL = 1024
NORM_EPS = 1e-6
CHUNK = 64

SSD_D_INNER = 2048
SSD_HEAD_DIM = 64
SSD_HEADS = 32
SSD_GROUPS = 8
SSD_HEADS_PER_GROUP = 4
SSD_STATE = 128
SSD_CONV = 4
SSD_GN = SSD_GROUPS * SSD_STATE
SSD_CONV_DIM = SSD_D_INNER + 2 * SSD_GN
SSD_ZX_DIM = SSD_D_INNER + SSD_CONV_DIM
SSD_GROUP_WIDTH = SSD_HEADS_PER_GROUP * SSD_HEAD_DIM

MLA_HEADS = 16
MLA_Q_LORA = 512
MLA_KV_LORA = 256
MLA_NOPE = 64
MLA_ROPE = 32
MLA_ROPE_HALF = MLA_ROPE // 2
MLA_QK = MLA_NOPE + MLA_ROPE
MLA_V = 64
MLA_SCALE = 1.0 / math.sqrt(MLA_QK)
MLA_SCALE_LOG2 = MLA_SCALE * math.log2(math.e)
ROPE_THETA = 10000.0

FFN_HIDDEN = 2816

LANES = 128
MXU_WIDTH = 256
VMEM_LIMIT_BYTES = 56 * 1024 * 1024

HEAD_SLOT = LANES
ROPE_LO = MLA_NOPE
ROPE_HI = MLA_NOPE + MLA_ROPE_HALF
MLA_Q_WIDTH = MLA_HEADS * HEAD_SLOT
MLA_V_WIDTH = MLA_HEADS * MLA_V
MASK_VALUE = -1e30


def _rms(x, g):
    return x * lax.rsqrt(jnp.mean(x * x, axis=-1, keepdims=True) + NORM_EPS) * g


def _silu_of_twice(half):
    return half + half * jnp.tanh(half)


def _silu(x):
    return _silu_of_twice(0.5 * x)


def _resident(shape):
    zeros = (0,) * len(shape)
    return pl.BlockSpec(shape, lambda *_: zeros, pipeline_mode=pl.Buffered(1))


def _params(*semantics):
    return pltpu.CompilerParams(dimension_semantics=semantics, vmem_limit_bytes=VMEM_LIMIT_BYTES)


def _row_tile(n, target):
    if n <= target:
        return n
    for t in range(target, 0, -16):
        if n % t == 0:
            return t
    raise ValueError((n, target))


INPROJ_COLS = 512


def _inproj_kernel(x_ref, g_ref, w_ref, wdt_ref, zx_ref, dt_ref):
    xn = _rms(x_ref[...], g_ref[...]).astype(BF16)
    for n in range(0, SSD_ZX_DIM, INPROJ_COLS):
        zx_ref[:, n:n + INPROJ_COLS] = jnp.dot(
            xn, w_ref[:, n:n + INPROJ_COLS], preferred_element_type=F32).astype(BF16)
    dt_ref[...] = jnp.dot(xn, wdt_ref[...], preferred_element_type=F32)


def _inproj(x, g, w_zx, w_dt):
    t = x.shape[0]
    tm = _row_tile(t, 512)
    return pl.pallas_call(
        _inproj_kernel,
        grid=(t // tm,),
        in_specs=[
            pl.BlockSpec((tm, D_MODEL), lambda i: (i, 0)),
            _resident((1, D_MODEL)),
            _resident((D_MODEL, SSD_ZX_DIM)),
            _resident((D_MODEL, LANES)),
        ],
        out_specs=[
            pl.BlockSpec((tm, SSD_ZX_DIM), lambda i: (i, 0)),
            pl.BlockSpec((tm, LANES), lambda i: (i, 0)),
        ],
        out_shape=[
            jax.ShapeDtypeStruct((t, SSD_ZX_DIM), BF16),
            jax.ShapeDtypeStruct((t, LANES), F32),
        ],
        compiler_params=_params("parallel"),
        name="ssd_inproj",
    )(x, g, w_zx, w_dt)


CONV_TAIL = SSD_CONV - 1
CONV_PAD = 8
CONV_GROUP = 16
CONV_COLS = 512
SSD_Q = LANES


def _ssd_kernel(zx_ref, dt_ref, h_ref, cinit_ref, sinit_ref, convw_ref, convb_ref, dtb_ref,
                alog_ref, dskip_ref, gn_ref, wout_ref, gpost_ref,
                hout_ref, cout_ref, sout_ref,
                tail_ref, head_ref, xc_ref, st_ref, y_ref, act_ref,
                *, tl, l_valid):
    t = pl.program_id(1)
    nt = pl.num_programs(1)
    q = SSD_Q
    gw = SSD_GROUP_WIDTH

    @pl.when(t == 0)
    def _():
        tail_ref[...] = jnp.zeros((CONV_PAD, SSD_CONV_DIM), F32)
        tail_ref[CONV_PAD - CONV_TAIL:CONV_PAD, :] = cinit_ref[...]
        for g in range(SSD_GROUPS):
            st_ref[g] = sinit_ref[g * gw:(g + 1) * gw, :].T

    rb = min(tl, MXU_WIDTH)
    srow = lax.broadcasted_iota(jnp.int32, (rb, rb), 0)
    scol = lax.broadcasted_iota(jnp.int32, (rb, rb), 1)
    shifts = [(srow - scol == j).astype(BF16) for j in range(1, SSD_CONV)]
    for c0 in range(0, SSD_CONV_DIM, CONV_COLS):
        cs = slice(c0, c0 + CONV_COLS)
        w_half = 0.5 * convw_ref[:, cs]
        b_half = 0.5 * convb_ref[:, cs]
        context = tail_ref[:, cs]
        for h0 in range(0, tl, rb):
            xb = zx_ref[h0:h0 + rb, SSD_D_INNER + c0:SSD_D_INNER + c0 + CONV_COLS]
            x = xb.astype(F32)
            acc = b_half + x * w_half[SSD_CONV - 1:SSD_CONV]
            for j in range(1, SSD_CONV):
                shifted = jnp.dot(shifts[j - 1], xb, preferred_element_type=F32)
                acc = acc + shifted * w_half[SSD_CONV - 1 - j:SSD_CONV - j]
            xc_ref[h0:h0 + rb, cs] = _silu_of_twice(acc)

            head_ref[0:CONV_PAD, :] = context
            head_ref[CONV_PAD:2 * CONV_PAD, :] = x[0:CONV_PAD]
            head = b_half
            for k in range(SSD_CONV):
                lo = CONV_PAD - CONV_TAIL + k
                head = head + head_ref[lo:lo + CONV_PAD, :] * w_half[k:k + 1]
            xc_ref[h0:h0 + CONV_PAD, cs] = _silu_of_twice(head)
            context = x[rb - CONV_PAD:rb]
        tail_ref[:, cs] = context

    a_row = -jnp.exp(alog_ref[...])
    rows = lax.broadcasted_iota(jnp.int32, (q, q), 0)
    cols = lax.broadcasted_iota(jnp.int32, (q, q), 1)
    causal = rows >= cols
    tri = causal.astype(F32)
    low_half = lax.broadcasted_iota(jnp.int32, (q, LANES), 1) < SSD_HEAD_DIM
    head_of_lane = lax.broadcasted_iota(jnp.int32, (q, gw), 1) // SSD_HEAD_DIM
    head_masks = [head_of_lane == r for r in range(SSD_HEADS_PER_GROUP)]

    def per_head_lanes(parts):
        return jnp.concatenate([jnp.where(low_half, parts[0], parts[1]),
                                jnp.where(low_half, parts[2], parts[3])], axis=1)

    def chunk(c):
        r0 = c * q
        rs = slice(r0, r0 + q)
        raw = dt_ref[rs, :] + dtb_ref[...]
        dt = jnp.maximum(raw, 0.0) + jnp.log(1.0 + jnp.exp(-jnp.abs(raw)))
        if l_valid is not None:
            row_id = t * tl + r0 + lax.broadcasted_iota(jnp.int32, (q, LANES), 0)
            dt = jnp.where(row_id < l_valid, dt, 0.0)
        da = dt * a_row
        acum = jnp.dot(tri, da, preferred_element_type=F32, precision=lax.Precision.HIGHEST)
        act_ref[c] = acum.T

        for g in range(SSD_GROUPS):
            b0 = SSD_D_INNER + g * SSD_STATE
            c0 = SSD_D_INNER + SSD_GN + g * SSD_STATE
            gs = slice(g * gw, (g + 1) * gw)
            bg = xc_ref[rs, b0:b0 + SSD_STATE].astype(BF16)
            cg = xc_ref[rs, c0:c0 + SSD_STATE].astype(BF16)
            xg = xc_ref[rs, gs]
            cb = lax.dot_general(cg, bg, (((1,), (1,)), ((), ())), preferred_element_type=F32)
            st = st_ref[g]
            y_state = jnp.dot(cg, st.astype(BF16), preferred_element_type=F32)

            heads = range(g * SSD_HEADS_PER_GROUP, (g + 1) * SSD_HEADS_PER_GROUP)
            acols = [jnp.broadcast_to(acum[:, h:h + 1], (q, LANES)) for h in heads]
            acol_g = per_head_lanes(acols)
            dcol_g = per_head_lanes([jnp.broadcast_to(dt[:, h:h + 1], (q, LANES)) for h in heads])
            xdt = xg * dcol_g

            y = jnp.exp(acol_g) * y_state + dskip_ref[:, gs] * xg
            for r, h in enumerate(heads):
                seg = jnp.where(causal, acols[r] - act_ref[c, h:h + 1, :], MASK_VALUE)
                m = (cb * jnp.exp(seg)).astype(BF16)
                own = jnp.where(head_masks[r], xdt, 0.0).astype(BF16)
                y = y + jnp.dot(m, own, preferred_element_type=F32)

            a_last = acol_g[q - 1:q, :]
            xw = (xdt * jnp.exp(a_last - acol_g)).astype(BF16)
            upd = lax.dot_general(bg, xw, (((0,), (0,)), ((), ())), preferred_element_type=F32)
            st_ref[g] = jnp.exp(a_last) * st + upd

            zg = zx_ref[rs, gs].astype(F32)
            y_ref[rs, gs] = _rms(y * _silu(zg), gn_ref[:, gs]).astype(BF16)

    for c in range(tl // q):
        chunk(c)

    mix = jnp.dot(y_ref[...], wout_ref[...], preferred_element_type=F32)
    hout_ref[...] = h_ref[...] + _rms(mix, gpost_ref[...])

    @pl.when(t == nt - 1)
    def _():
        n_rows = tl if l_valid is None else (l_valid - 1) % tl + 1
        group = zx_ref[n_rows - CONV_GROUP:n_rows, SSD_D_INNER:].astype(F32)
        cout_ref[...] = group[CONV_GROUP - CONV_TAIL:CONV_GROUP]
        for g in range(SSD_GROUPS):
            sout_ref[g * gw:(g + 1) * gw, :] = st_ref[g].T


def _ssd(zx, dt_raw, h, conv_init, state_init, conv_w, conv_b, dt_bias, a_log, d_skip, gate_norm,
         w_out, g_post, *, l_valid):
    b, l, _ = zx.shape
    tl = _row_tile(l, 512)
    assert tl % SSD_Q == 0 and l - l_valid < tl
    hp = SSD_HEADS * SSD_HEAD_DIM
    kern = functools.partial(_ssd_kernel, tl=tl, l_valid=None if l_valid == l else l_valid)
    return pl.pallas_call(
        kern,
        grid=(b, l // tl),
        in_specs=[
            pl.BlockSpec((None, tl, SSD_ZX_DIM), lambda i, j: (i, j, 0)),
            pl.BlockSpec((None, tl, LANES), lambda i, j: (i, j, 0)),
            pl.BlockSpec((None, tl, D_MODEL), lambda i, j: (i, j, 0)),
            pl.BlockSpec((None, CONV_TAIL, SSD_CONV_DIM), lambda i, j: (i, 0, 0)),
            pl.BlockSpec((None, hp, SSD_STATE), lambda i, j: (i, 0, 0)),
            _resident((SSD_CONV, SSD_CONV_DIM)),
            _resident((1, SSD_CONV_DIM)),
            _resident((1, LANES)),
            _resident((1, LANES)),
            _resident((1, SSD_D_INNER)),
            _resident((1, SSD_D_INNER)),
            _resident((SSD_D_INNER, D_MODEL)),
            _resident((1, D_MODEL)),
        ],
        out_specs=[
            pl.BlockSpec((None, tl, D_MODEL), lambda i, j: (i, j, 0)),
            pl.BlockSpec((None, CONV_TAIL, SSD_CONV_DIM), lambda i, j: (i, 0, 0)),
            pl.BlockSpec((None, hp, SSD_STATE), lambda i, j: (i, 0, 0)),
        ],
        out_shape=[
            jax.ShapeDtypeStruct((b, l, D_MODEL), F32),
            jax.ShapeDtypeStruct((b, CONV_TAIL, SSD_CONV_DIM), F32),
            jax.ShapeDtypeStruct((b, hp, SSD_STATE), F32),
        ],
        scratch_shapes=[
            pltpu.VMEM((CONV_PAD, SSD_CONV_DIM), F32),
            pltpu.VMEM((2 * CONV_PAD, CONV_COLS), F32),
            pltpu.VMEM((tl, SSD_CONV_DIM), F32),
            pltpu.VMEM((SSD_GROUPS, SSD_STATE, SSD_GROUP_WIDTH), F32),
            pltpu.VMEM((tl, SSD_D_INNER), BF16),
            pltpu.VMEM((tl // SSD_Q, LANES, SSD_Q), F32),
        ],
        compiler_params=_params("parallel", "arbitrary"),
        name="ssd_scan",
    )(zx, dt_raw, h, conv_init, state_init, conv_w, conv_b, dt_bias, a_log, d_skip, gate_norm,
      w_out, g_post)


FFN_COLS = MXU_WIDTH


def _ffn_kernel(*refs, mixed):
    if mixed:
        a_ref, wmix_ref, gmix_ref, h_ref, gpre_ref, wg_ref, wu_ref, wd_ref, gpost_ref, out_ref = refs
        mix = jnp.dot(a_ref[...], wmix_ref[...], preferred_element_type=F32)
        h = h_ref[...] + _rms(mix, gmix_ref[...])
    else:
        h_ref, gpre_ref, wg_ref, wu_ref, wd_ref, gpost_ref, out_ref = refs
        h = h_ref[...]
    xn = _rms(h, gpre_ref[...]).astype(BF16)
    acc = None
    for f in range(0, FFN_HIDDEN, FFN_COLS):
        gate = jnp.dot(xn, wg_ref[:, f:f + FFN_COLS], preferred_element_type=F32)
        up = jnp.dot(xn, wu_ref[:, f:f + FFN_COLS], preferred_element_type=F32)
        act = (_silu(gate) * up).astype(BF16)
        part = jnp.dot(act, wd_ref[f:f + FFN_COLS, :], preferred_element_type=F32)
        acc = part if acc is None else acc + part
    out_ref[...] = h + _rms(acc, gpost_ref[...])


def _ffn(h, g_pre, w_gate, w_up, w_down, g_post, mix=None):
    t = h.shape[0]
    tm = _row_tile(t, 512)
    rows = lambda width: pl.BlockSpec((tm, width), lambda i: (i, 0))
    mix_specs, mix_args = [], []
    if mix is not None:
        a, w_mix, g_mix = mix
        mix_specs = [rows(a.shape[1]), _resident(w_mix.shape), _resident((1, D_MODEL))]
        mix_args = [a, w_mix, g_mix]
    return pl.pallas_call(
        functools.partial(_ffn_kernel, mixed=mix is not None),
        grid=(t // tm,),
        in_specs=mix_specs + [
            rows(D_MODEL),
            _resident((1, D_MODEL)),
            _resident((D_MODEL, FFN_HIDDEN)),
            _resident((D_MODEL, FFN_HIDDEN)),
            _resident((FFN_HIDDEN, D_MODEL)),
            _resident((1, D_MODEL)),
        ],
        out_specs=rows(D_MODEL),
        out_shape=jax.ShapeDtypeStruct((t, D_MODEL), F32),
        compiler_params=_params("parallel"),
        name="mix_ffn" if mix is not None else "ffn",
    )(*mix_args, h, g_pre, w_gate, w_up, w_down, g_post)


def _rope_slot(x, cos, sin_hi, sin_lo):
    return (x * cos + pltpu.roll(x, MLA_ROPE_HALF, axis=1) * sin_hi
            + pltpu.roll(x, LANES - MLA_ROPE_HALF, axis=1) * sin_lo)


def _mla_pre_kernel(h_ref, gpre_ref, wqa_ref, qn_ref, wqb_ref, wlat_ref, kvn_ref, wkr_ref,
                    cos_ref, shi_ref, slo_ref, q_ref, lat_ref, kr_ref):
    xn = _rms(h_ref[...], gpre_ref[...]).astype(BF16)
    cos, s_hi, s_lo = cos_ref[...], shi_ref[...], slo_ref[...]

    qa = jnp.dot(xn, wqa_ref[...], preferred_element_type=F32)
    qa = _rms(qa, qn_ref[...]).astype(BF16)
    for c0 in range(0, MLA_Q_WIDTH, MXU_WIDTH):
        qb = jnp.dot(qa, wqb_ref[:, c0:c0 + MXU_WIDTH], preferred_element_type=F32)
        for s0 in range(0, MXU_WIDTH, HEAD_SLOT):
            qh = _rope_slot(qb[:, s0:s0 + HEAD_SLOT], cos, s_hi, s_lo) * MLA_SCALE_LOG2
            q_ref[:, c0 + s0:c0 + s0 + HEAD_SLOT] = qh.astype(BF16)

    lat = jnp.dot(xn, wlat_ref[...], preferred_element_type=F32)
    lat_ref[...] = _rms(lat, kvn_ref[...])
    kr = _rope_slot(jnp.dot(xn, wkr_ref[...], preferred_element_type=F32), cos, s_hi, s_lo)
    kr_ref[...] = pltpu.roll(kr, LANES - ROPE_LO, axis=1)[:, :MLA_ROPE]


def _mla_pre(h, g_pre, wq_a, q_norm, wq_b, w_lat, kv_norm, w_kr, cos, s_hi, s_lo):
    b, l, _ = h.shape
    tm = _row_tile(l, 512)
    row = lambda i, j: (i, j, 0)
    tab = pl.BlockSpec((tm, LANES), lambda i, j: (j, 0))
    return pl.pallas_call(
        _mla_pre_kernel,
        grid=(b, l // tm),
        in_specs=[
            pl.BlockSpec((None, tm, D_MODEL), row),
            _resident((1, D_MODEL)),
            _resident((D_MODEL, MLA_Q_LORA)),
            _resident((1, MLA_Q_LORA)),
            _resident((MLA_Q_LORA, MLA_Q_WIDTH)),
            _resident((D_MODEL, MLA_KV_LORA)),
            _resident((1, MLA_KV_LORA)),
            _resident((D_MODEL, LANES)),
            tab, tab, tab,
        ],
        out_specs=[
            pl.BlockSpec((None, tm, MLA_Q_WIDTH), row),
            pl.BlockSpec((None, tm, MLA_KV_LORA), row),
            pl.BlockSpec((None, tm, MLA_ROPE), row),
        ],
        out_shape=[
            jax.ShapeDtypeStruct((b, l, MLA_Q_WIDTH), BF16),
            jax.ShapeDtypeStruct((b, l, MLA_KV_LORA), F32),
            jax.ShapeDtypeStruct((b, l, MLA_ROPE), F32),
        ],
        compiler_params=_params("parallel", "parallel"),
        name="mla_pre",
    )(h, g_pre, wq_a, q_norm, wq_b, w_lat, kv_norm, w_kr, cos, s_hi, s_lo)


def _kv_expand_kernel(lat_ref, kr_ref, wk_ref, wv_ref, place_ref, k_ref, v_ref):
    lat = lat_ref[...].astype(BF16)
    kr_slot = jnp.dot(kr_ref[...].astype(BF16), place_ref[...], preferred_element_type=F32)
    ones_lane = (lax.broadcasted_iota(jnp.int32, (1, HEAD_SLOT), 1) == MLA_V).astype(F32)
    kr_pair = jnp.concatenate([kr_slot, kr_slot], axis=1)
    ones_pair = jnp.concatenate([ones_lane, ones_lane], axis=1)
    for c0 in range(0, MLA_Q_WIDTH, MXU_WIDTH):
        sl = slice(c0, c0 + MXU_WIDTH)
        kb = jnp.dot(lat, wk_ref[:, sl], preferred_element_type=F32)
        k_ref[:, sl] = (kb + kr_pair).astype(BF16)
        vb = jnp.dot(lat, wv_ref[:, sl], preferred_element_type=F32)
        v_ref[:, sl] = (vb + ones_pair).astype(BF16)


def _kv_expand(lat, kr, w_k, w_v, place):
    t = lat.shape[0]
    tm = _row_tile(t, 512)
    return pl.pallas_call(
        _kv_expand_kernel,
        grid=(t // tm,),
        in_specs=[
            pl.BlockSpec((tm, MLA_KV_LORA), lambda i: (i, 0)),
            pl.BlockSpec((tm, MLA_ROPE), lambda i: (i, 0)),
            _resident((MLA_KV_LORA, MLA_Q_WIDTH)),
            _resident((MLA_KV_LORA, MLA_Q_WIDTH)),
            _resident((MLA_ROPE, LANES)),
        ],
        out_specs=[
            pl.BlockSpec((tm, MLA_Q_WIDTH), lambda i: (i, 0)),
            pl.BlockSpec((tm, MLA_Q_WIDTH), lambda i: (i, 0)),
        ],
        out_shape=[
            jax.ShapeDtypeStruct((t, MLA_Q_WIDTH), BF16),
            jax.ShapeDtypeStruct((t, MLA_Q_WIDTH), BF16),
        ],
        compiler_params=_params("parallel"),
        name="mla_kv_expand",
    )(lat, kr, w_k, w_v, place)


ATTN_SUB_K = 256
ATTN_SMALL_Q = 64
ATTN_FULL, ATTN_MASKED, ATTN_DIAGONAL = 0, 1, 2


def _kv_tiles_needed(qi, *, tq, tk, q_off, lk):
    last_chunk = (q_off + qi * tq + tq - 1) // CHUNK
    k_end = min((last_chunk + 1) * CHUNK, lk)
    return (k_end + tk - 1) // tk


def _attn_kernel(qi_ref, ki_ref, last_ref, mode_ref, q_ref, k_ref, v_ref, o_ref, m_ref, acc_ref,
                 *, tq, tk, sub_k, q_off, lk, variants):
    step_id = pl.program_id(1)
    qi = qi_ref[step_id]
    ki = ki_ref[step_id]

    @pl.when(ki == 0)
    def _():
        m_ref[...] = jnp.full(m_ref.shape, MASK_VALUE, F32)
        acc_ref[...] = jnp.zeros(acc_ref.shape, F32)

    def sub_steps(kind, lead):
        for k0 in range(0, tk, sub_k):
            if kind == ATTN_FULL:
                yield k0, 0, False
            elif kind == ATTN_MASKED:
                yield k0, 0, True
            elif max(0, k0 - lead) < tq:
                yield k0, max(0, k0 - lead), k0 + sub_k > lead + CHUNK

    def step(kind, lead):
        if any(masked for _, _, masked in sub_steps(kind, lead)):
            q_chunk = (q_off + qi * tq + lax.broadcasted_iota(jnp.int32, (tq, tk), 0)) // CHUNK
            k_pos = ki * tk + lax.broadcasted_iota(jnp.int32, (tq, tk), 1)
            visible = jnp.logical_and(k_pos // CHUNK <= q_chunk, k_pos < lk)
        for hd in range(MLA_HEADS):
            sl = slice(hd * HEAD_SLOT, (hd + 1) * HEAD_SLOT)
            for k0, r0, masked in sub_steps(kind, lead):
                ks = slice(k0, k0 + sub_k)
                rs = slice(r0, tq)
                s = lax.dot_general(q_ref[rs, sl], k_ref[ks, sl], (((1,), (1,)), ((), ())),
                                    preferred_element_type=F32)
                if masked:
                    s = jnp.where(visible[rs, ks], s, MASK_VALUE)
                m_prev = m_ref[hd, rs, :]
                m_new = jnp.maximum(m_prev, jnp.max(s, axis=-1, keepdims=True))
                alpha = jnp.exp2((m_prev - m_new).astype(BF16)).astype(F32)
                p = jnp.exp2((s - jnp.tile(m_new, (1, sub_k // LANES))).astype(BF16))
                m_ref[hd, rs, :] = m_new
                pv = jnp.dot(p, v_ref[ks, sl], preferred_element_type=F32)
                acc_ref[hd, rs, :] = alpha * acc_ref[hd, rs, :] + pv

    for index, (kind, lead) in enumerate(variants):
        pl.when(mode_ref[step_id] == index)(functools.partial(step, kind, lead))

    @pl.when(last_ref[step_id] == 1)
    def _():
        low_half = lax.broadcasted_iota(jnp.int32, (tq, LANES), 1) < MLA_V
        for pair in range(MLA_HEADS // 2):
            even = acc_ref[2 * pair]
            odd = acc_ref[2 * pair + 1]
            even = even / even[:, MLA_V:MLA_V + 1]
            odd = odd / odd[:, MLA_V:MLA_V + 1]
            packed = jnp.where(low_half, even, pltpu.roll(odd, MLA_V, axis=1))
            o_ref[:, pair * LANES:(pair + 1) * LANES] = packed.astype(BF16)


def _attention(q, k, v, *, q_off, lk):
    b, lq, _ = q.shape
    lk_pad = k.shape[1]
    tq = _row_tile(lq, 1024)
    tk = lk_pad if lk_pad <= 1536 else 1024
    assert lk_pad % tk == 0 and tk % LANES == 0
    static = dict(tq=tq, tk=tk, q_off=q_off, lk=lk)

    sub_k = ATTN_SUB_K if tq > ATTN_SMALL_Q else tk

    def tile_variant(qi, ki):
        q_start, k_start = q_off + qi * tq, ki * tk
        inside = k_start + tk <= lk
        if inside and (k_start + tk - 1) // CHUNK <= q_start // CHUNK:
            return (ATTN_FULL, 0)
        if inside and q_start % CHUNK == 0 and k_start % CHUNK == 0 and sub_k % CHUNK == 0:
            return (ATTN_DIAGONAL, q_start - k_start)
        return (ATTN_MASKED, 0)

    qi_tab, ki_tab, last_tab, mode_tab, variants = [], [], [], [], []
    for qi in range(lq // tq):
        needed = _kv_tiles_needed(qi, **static)
        qi_tab += [qi] * needed
        ki_tab += list(range(needed))
        last_tab += [0] * (needed - 1) + [1]
        for ki in range(needed):
            variant = tile_variant(qi, ki)
            if variant not in variants:
                variants.append(variant)
            mode_tab.append(variants.index(variant))
    tables = [jnp.asarray(tab, jnp.int32) for tab in (qi_tab, ki_tab, last_tab, mode_tab)]

    q_index = lambda i, s, qi_ref, ki_ref, last_ref, mode_ref: (i, qi_ref[s], 0)
    k_index = lambda i, s, qi_ref, ki_ref, last_ref, mode_ref: (i, ki_ref[s], 0)
    return pl.pallas_call(
        functools.partial(_attn_kernel, sub_k=sub_k, variants=tuple(variants), **static),
        grid_spec=pltpu.PrefetchScalarGridSpec(
            num_scalar_prefetch=4,
            grid=(b, len(qi_tab)),
            in_specs=[
                pl.BlockSpec((None, tq, MLA_Q_WIDTH), q_index),
                pl.BlockSpec((None, tk, MLA_Q_WIDTH), k_index),
                pl.BlockSpec((None, tk, MLA_Q_WIDTH), k_index),
            ],
            out_specs=pl.BlockSpec((None, tq, MLA_V_WIDTH), q_index),
            scratch_shapes=[
                pltpu.VMEM((MLA_HEADS, tq, LANES), F32),
                pltpu.VMEM((MLA_HEADS, tq, LANES), F32),
            ],
        ),
        out_shape=jax.ShapeDtypeStruct((b, lq, MLA_V_WIDTH), BF16),
        compiler_params=_params("parallel", "arbitrary"),
        name="mla_attention",
    )(*tables, q, k, v)


def _prep_params(ln_mix_pre, ln_mix_post, ln_ffn_pre, ln_ffn_post,
                 ssd_w_in, ssd_conv_w, ssd_conv_b, ssd_dt_bias, ssd_a_log, ssd_d, ssd_gate_norm, ssd_w_out,
                 mla_wq_a, mla_q_norm, mla_wq_b, mla_wkv_a, mla_kv_norm, mla_w_uk, mla_w_uv, mla_w_o,
                 ffn_w_gate, ffn_w_up, ffn_w_down):
    row = lambda v: v.reshape(1, -1).astype(F32)
    pad_lanes = lambda v: jnp.pad(v, ((0, 0), (0, LANES - v.shape[1])))
    head_slots = lambda w, d: jnp.pad(
        w.reshape(w.shape[0], MLA_HEADS, d), ((0, 0), (0, 0), (0, HEAD_SLOT - d))).reshape(w.shape[0], -1)
    w_in = ssd_w_in[0]
    w_kr = jnp.zeros((D_MODEL, LANES), F32).at[:, ROPE_LO:ROPE_LO + MLA_ROPE].set(mla_wkv_a[0][:, MLA_KV_LORA:])
    place = jnp.zeros((MLA_ROPE, LANES), F32).at[jnp.arange(MLA_ROPE), ROPE_LO + jnp.arange(MLA_ROPE)].set(1.0)
    return dict(
        ln_mix_pre=[row(ln_mix_pre[i]) for i in range(2)],
        ln_mix_post=[row(ln_mix_post[i]) for i in range(2)],
        ln_ffn_pre=[row(ln_ffn_pre[i]) for i in range(2)],
        ln_ffn_post=[row(ln_ffn_post[i]) for i in range(2)],
        w_zx=w_in[:, :SSD_ZX_DIM].astype(BF16),
        w_dt=pad_lanes(w_in[:, SSD_ZX_DIM:]).astype(BF16),
        conv_w=ssd_conv_w[0].astype(F32),
        conv_b=row(ssd_conv_b[0]),
        dt_bias=pad_lanes(row(ssd_dt_bias[0])),
        a_log=pad_lanes(row(ssd_a_log[0])),
        d_skip=row(jnp.repeat(ssd_d[0], SSD_HEAD_DIM)),
        gate_norm=row(ssd_gate_norm[0]),
        w_out=ssd_w_out[0].astype(BF16),
        wq_a=mla_wq_a[0].astype(BF16),
        q_norm=row(mla_q_norm[0]),
        wq_b=head_slots(mla_wq_b[0], MLA_QK).astype(BF16),
        w_lat=mla_wkv_a[0][:, :MLA_KV_LORA].astype(BF16),
        kv_norm=row(mla_kv_norm[0]),
        w_kr=w_kr.astype(BF16),
        w_k=head_slots(mla_w_uk[0].reshape(MLA_KV_LORA, -1), MLA_NOPE).astype(BF16),
        w_v=head_slots(mla_w_uv[0].reshape(MLA_KV_LORA, -1), MLA_V).astype(BF16),
        place=place.astype(BF16),
        w_o=mla_w_o[0].astype(BF16),
        ffn_w_gate=[ffn_w_gate[i].astype(BF16) for i in range(2)],
        ffn_w_up=[ffn_w_up[i].astype(BF16) for i in range(2)],
        ffn_w_down=[ffn_w_down[i].astype(BF16) for i in range(2)],
    )


def _rope_tables(pos):
    inv = ROPE_THETA ** (-jnp.arange(MLA_ROPE_HALF, dtype=F32) / MLA_ROPE_HALF)
    ang = pos.astype(F32)[:, None] * inv[None, :]
    cos, sin = jnp.cos(ang), jnp.sin(ang)
    l = pos.shape[0]
    ones = jnp.ones((l, ROPE_LO), F32)
    tail = jnp.ones((l, LANES - ROPE_LO - MLA_ROPE), F32)
    zeros_lo = jnp.zeros((l, ROPE_LO), F32)
    zeros_half = jnp.zeros((l, MLA_ROPE_HALF), F32)
    zeros_tail = jnp.zeros((l, LANES - ROPE_LO - MLA_ROPE), F32)
    cos_t = jnp.concatenate([ones, cos, cos, tail], axis=1)
    s_hi = jnp.concatenate([zeros_lo, zeros_half, sin, zeros_tail], axis=1)
    s_lo = jnp.concatenate([zeros_lo, -sin, zeros_half, zeros_tail], axis=1)
    return cos_t, s_hi, s_lo


def _round_up(n, m):
    return -(-n // m) * m


def _trunk(x, conv_init, state_init, past_lat, past_kr, p):
    b, l, _ = x.shape
    t = b * l
    past = past_lat.shape[1]
    flat = lambda a: a.reshape(t, a.shape[-1])

    zx, dt_raw = _inproj(flat(x), p['ln_mix_pre'][0], p['w_zx'], p['w_dt'])
    l_pad = _round_up(l, SSD_Q)
    seq = lambda a: jnp.pad(a.reshape(b, l, a.shape[-1]), ((0, 0), (0, l_pad - l), (0, 0)))
    h1, new_conv, new_state = _ssd(
        seq(zx), seq(dt_raw), seq(flat(x)), conv_init, state_init.reshape(b, -1, SSD_STATE),
        p['conv_w'], p['conv_b'], p['dt_bias'], p['a_log'], p['d_skip'], p['gate_norm'],
        p['w_out'], p['ln_mix_post'][0], l_valid=l)
    h1 = flat(h1[:, :l])
    h2 = _ffn(h1, p['ln_ffn_pre'][0], p['ffn_w_gate'][0], p['ffn_w_up'][0], p['ffn_w_down'][0],
              p['ln_ffn_post'][0])

    cos, s_hi, s_lo = _rope_tables(past + jnp.arange(l, dtype=jnp.int32))
    q, lat, kr = _mla_pre(h2.reshape(b, l, D_MODEL), p['ln_mix_pre'][1], p['wq_a'], p['q_norm'], p['wq_b'],
                          p['w_lat'], p['kv_norm'], p['w_kr'], cos, s_hi, s_lo)
    lk = past + l
    lk_pad = _round_up(lk, ATTN_SUB_K if l > ATTN_SMALL_Q else LANES)
    keys = lambda new, old: jnp.pad(jnp.concatenate([old.astype(F32), new], axis=1),
                                    ((0, 0), (0, lk_pad - lk), (0, 0)))
    k_all, v_all = _kv_expand(keys(lat, past_lat).reshape(b * lk_pad, -1),
                              keys(kr, past_kr).reshape(b * lk_pad, -1), p['w_k'], p['w_v'], p['place'])
    o = _attention(q, k_all.reshape(b, lk_pad, -1), v_all.reshape(b, lk_pad, -1), q_off=past, lk=lk)
    h4 = _ffn(h2, p['ln_ffn_pre'][1], p['ffn_w_gate'][1], p['ffn_w_up'][1], p['ffn_w_down'][1],
              p['ln_ffn_post'][1], mix=(flat(o), p['w_o'], p['ln_mix_post'][1]))

    return (h4.reshape(b, l, D_MODEL), new_conv[None],
            new_state.reshape(b, SSD_HEADS, SSD_HEAD_DIM, SSD_STATE)[None], lat[None], kr[None])


def kernel(x_prompt, x_sample, state_ssd_conv, state_ssd_ssm, cache_mla_latent, cache_mla_krope,
           ln_mix_pre, ln_mix_post, ln_ffn_pre, ln_ffn_post,
           ssd_w_in, ssd_conv_w, ssd_conv_b, ssd_dt_bias, ssd_a_log, ssd_d, ssd_gate_norm, ssd_w_out,
           mla_wq_a, mla_q_norm, mla_wq_b, mla_wkv_a, mla_kv_norm, mla_w_uk, mla_w_uv, mla_w_o,
           ffn_w_gate, ffn_w_up, ffn_w_down):
    p = _prep_params(ln_mix_pre, ln_mix_post, ln_ffn_pre, ln_ffn_post,
                     ssd_w_in, ssd_conv_w, ssd_conv_b, ssd_dt_bias, ssd_a_log, ssd_d, ssd_gate_norm,
                     ssd_w_out, mla_wq_a, mla_q_norm, mla_wq_b, mla_wkv_a, mla_kv_norm, mla_w_uk,
                     mla_w_uv, mla_w_o, ffn_w_gate, ffn_w_up, ffn_w_down)
    b = x_prompt.shape[0]
    y_p, p_conv, p_ssm, p_lat, p_kr = _trunk(
        x_prompt,
        jnp.zeros((b, CONV_TAIL, SSD_CONV_DIM), F32),
        jnp.zeros((b, SSD_HEADS, SSD_HEAD_DIM, SSD_STATE), F32),
        jnp.zeros((b, 0, MLA_KV_LORA), F32),
        jnp.zeros((b, 0, MLA_ROPE), F32), p)
    y_s, s_conv, s_ssm, s_lat, s_kr = _trunk(
        x_sample, state_ssd_conv[0], state_ssd_ssm[0], cache_mla_latent[0], cache_mla_krope[0], p)
    return (y_p, y_s, p_conv, p_ssm, p_lat, p_kr, s_conv, s_ssm, s_lat, s_kr)
```
